```python
import math
import jax, jax.numpy as jnp
from jax import lax
import numpy as np

D_MODEL = 2048
BATCH = 16
SEQ = 2048
DEPTH = 2
DEC_BATCH = 8
DEC_SEQ = 64
PAST_LEN = 4096

CHUNK = 64
Q_BLOCK = 128
EPS = 1e-6

SSD_HEADS = 32
SSD_HEAD_DIM = 64
SSD_WIDTH = SSD_HEADS * SSD_HEAD_DIM
SSD_GROUPS = 4
SSD_STATE = 128
CONV_W = 4
CONV_CH = SSD_WIDTH + 2 * SSD_GROUPS * SSD_STATE

MLA_HEADS = 16
MLA_NOPE = 128
MLA_ROPE = 64
MLA_QK = MLA_NOPE + MLA_ROPE
MLA_V = 128
MLA_WIDTH = MLA_HEADS * MLA_V
KV_LORA = 512
ROPE_THETA = 10000.0

SB_HEADS = 16
SB_HEAD_DIM = 128
SB_WIDTH = SB_HEADS * SB_HEAD_DIM

N_BRANCH = 3
BRANCH_WIDTH = 2048

IN_SIZES = (SSD_WIDTH, CONV_CH, SSD_HEADS,
            MLA_HEADS * MLA_QK, KV_LORA, MLA_ROPE, MLA_WIDTH,
            SB_WIDTH, SB_WIDTH, SB_WIDTH, SB_WIDTH,
            N_BRANCH * D_MODEL)
IN_TOTAL = (SSD_WIDTH + CONV_CH + SSD_HEADS + MLA_HEADS * MLA_QK + KV_LORA + MLA_ROPE
            + MLA_WIDTH + 4 * SB_WIDTH + N_BRANCH * D_MODEL)

kernel_name = "hybrid_ssd_mla_stickbreaking_stream_step"


def rms_norm(x, g):
    xf = x.astype(jnp.float32)
    y = xf * lax.rsqrt(jnp.mean(xf * xf, axis=-1, keepdims=True) + EPS)
    return (y * g.astype(jnp.float32)).astype(x.dtype)


def split_columns(proj):
    offs, acc = [], 0
    for s in IN_SIZES[:-1]:
        acc += s
        offs.append(acc)
    return jnp.split(proj, offs, axis=-1)


def rope(x, pos):
    half = MLA_ROPE // 2
    freq = ROPE_THETA ** (-jnp.arange(half, dtype=jnp.float32) / half)
    ang = pos.astype(jnp.float32)[:, None] * freq[None, :]
    shape = (1, pos.shape[0]) + (1,) * (x.ndim - 3) + (half,)
    cos, sin = jnp.cos(ang).reshape(shape), jnp.sin(ang).reshape(shape)
    x1 = x[..., :half].astype(jnp.float32)
    x2 = x[..., half:].astype(jnp.float32)
    return jnp.concatenate([x1 * cos - x2 * sin, x2 * cos + x1 * sin], axis=-1).astype(x.dtype)


def causal_conv(xbc, ctx, w, bias):
    T = xbc.shape[1]
    padded = jnp.concatenate([ctx.astype(xbc.dtype), xbc], axis=1)
    out = bias
    for j in range(CONV_W):
        out = out + w[j] * padded[:, j:j + T]
    return jax.nn.silu(out), padded[:, T:]


def ssd_scan(x, dt, a, bmat, cmat, h0):
    b, T = x.shape[:2]
    L = min(CHUNK, T)
    nc = T // L
    hg = SSD_HEADS // SSD_GROUPS
    xc = x.reshape(b, nc, L, SSD_GROUPS, hg, SSD_HEAD_DIM).astype(jnp.float32)
    dtc = dt.reshape(b, nc, L, SSD_GROUPS, hg)
    bc = bmat.reshape(b, nc, L, SSD_GROUPS, SSD_STATE).astype(jnp.float32)
    cc = cmat.reshape(b, nc, L, SSD_GROUPS, SSD_STATE).astype(jnp.float32)
    cum = jnp.cumsum(dtc * a.reshape(SSD_GROUPS, hg), axis=2)
    diff = cum[:, :, :, None] - cum[:, :, None, :]
    tri = jnp.tril(jnp.ones((L, L), dtype=bool))[None, None, :, :, None, None]
    decay = jnp.exp(jnp.where(tri, diff, -jnp.inf))
    cb = jnp.einsum('bctgn,bcsgn->bctsg', cc, bc)
    w = cb[..., None] * decay * dtc[:, :, None]
    y_diag = jnp.einsum('bctsgh,bcsghp->bctghp', w, xc)
    w_end = jnp.exp(cum[:, :, -1:] - cum) * dtc
    states = jnp.einsum('bcsgn,bcsgh,bcsghp->bcghpn', bc, w_end, xc)
    chunk_decay = jnp.exp(cum[:, :, -1])

    def step(h, inp):
        dec, st = inp
        return dec[..., None, None] * h + st, h

    h_init = h0.reshape(b, SSD_GROUPS, hg, SSD_HEAD_DIM, SSD_STATE).astype(jnp.float32)
    h_last, h_prev = lax.scan(step, h_init, (jnp.moveaxis(chunk_decay, 1, 0), jnp.moveaxis(states, 1, 0)))
    h_prev = jnp.moveaxis(h_prev, 0, 1)
    y_off = jnp.einsum('bctgn,bcghpn->bctghp', cc, h_prev) * jnp.exp(cum)[..., None]
    y = (y_diag + y_off).reshape(b, T, SSD_HEADS, SSD_HEAD_DIM)
    return y, h_last.reshape(b, SSD_HEADS, SSD_HEAD_DIM, SSD_STATE)


def chunk_causal_attention(q, k, v, q_pos, k_pos):
    T, S = q.shape[1], k.shape[1]
    qb = min(Q_BLOCK, T)
    past = S - T
    scale = 1.0 / math.sqrt(q.shape[-1])
    outs = []
    for i in range(T // qb):
        end = past + (i + 1) * qb
        s = jnp.einsum('bqhd,bkhd->bhqk', q[:, i * qb:(i + 1) * qb], k[:, :end]).astype(jnp.float32) * scale
        mask = (k_pos[:end] // CHUNK)[None, :] <= (q_pos[i * qb:(i + 1) * qb] // CHUNK)[:, None]
        p = jax.nn.softmax(jnp.where(mask, s, -jnp.inf), axis=-1).astype(v.dtype)
        outs.append(jnp.einsum('bhqk,bkhd->bqhd', p, v[:, :end]))
    return jnp.concatenate(outs, axis=1)


def stick_breaking_attention(q, k, v, q_pos, k_pos):
    T, S = q.shape[1], k.shape[1]
    qb = min(Q_BLOCK, T)
    past = S - T
    scale = 1.0 / math.sqrt(q.shape[-1])
    outs = []
    for i in range(T // qb):
        end = past + (i + 1) * qb
        z = jnp.einsum('bqhd,bkhd->bhqk', q[:, i * qb:(i + 1) * qb], k[:, :end]).astype(jnp.float32) * scale
        mask = k_pos[:end][None, :] < q_pos[i * qb:(i + 1) * qb][:, None]
        log_1m = jnp.where(mask, jax.nn.log_sigmoid(-z), 0.0)
        after = lax.cumsum(log_1m, axis=3, reverse=True) - log_1m
        att = jnp.where(mask, jnp.exp(jax.nn.log_sigmoid(z) + after), 0.0).astype(v.dtype)
        outs.append(jnp.einsum('bhqk,bkhd->bqhd', att, v[:, :end]))
    return jnp.concatenate(outs, axis=1)


def trunk_layer(h, sb_k_past, sb_v_past, ckv_past, krope_past, ssd_h0, conv_ctx,
                norm_g, w_in, conv_w, conv_b, dt_bias, a_log, d_skip, ssd_norm_g,
                kv_norm_g, w_ukv, q_norm_g, k_norm_g, w_branch, b_merge, w_out):
    b, T, _ = h.shape
    past = sb_k_past.shape[1]
    pos = past + jnp.arange(T)
    k_pos = jnp.arange(past + T)
    u = rms_norm(h, norm_g)
    (z, xbc, dt_raw, q_m, ckv, krope, g_m, q_s, k_s, v_s, g_s, gate_logit) = split_columns(u @ w_in)

    xbc, conv_new = causal_conv(xbc, conv_ctx, conv_w, conv_b)
    xs, bm, cm = jnp.split(xbc, [SSD_WIDTH, SSD_WIDTH + SSD_GROUPS * SSD_STATE], axis=-1)
    xs = xs.reshape(b, T, SSD_HEADS, SSD_HEAD_DIM)
    dt = jax.nn.softplus(dt_raw.astype(jnp.float32) + dt_bias.astype(jnp.float32))
    a = -jnp.exp(a_log.astype(jnp.float32))
    y, ssd_new = ssd_scan(xs, dt, a, bm.reshape(b, T, SSD_GROUPS, SSD_STATE),
                          cm.reshape(b, T, SSD_GROUPS, SSD_STATE), ssd_h0)
    y = (y + d_skip.astype(jnp.float32)[:, None] * xs.astype(jnp.float32)).astype(h.dtype)
    y = y.reshape(b, T, SSD_WIDTH) * jax.nn.silu(z)
    y_a = rms_norm(y.reshape(b, T, SSD_GROUPS, -1), ssd_norm_g.reshape(SSD_GROUPS, -1)).reshape(b, T, SSD_WIDTH)

    ckv = rms_norm(ckv, kv_norm_g)
    krope = rope(krope, pos)
    ckv_all = jnp.concatenate([ckv_past.astype(ckv.dtype), ckv], axis=1)
    krope_all = jnp.concatenate([krope_past.astype(krope.dtype), krope], axis=1)
    S = ckv_all.shape[1]
    kv = (ckv_all @ w_ukv).reshape(b, S, MLA_HEADS, MLA_NOPE + MLA_V)
    k_nope, v_m = jnp.split(kv, [MLA_NOPE], axis=-1)
    k_m = jnp.concatenate([k_nope, jnp.broadcast_to(krope_all[:, :, None, :], (b, S, MLA_HEADS, MLA_ROPE))], axis=-1)
    k_m = rms_norm(k_m, k_norm_g)
    q_m = q_m.reshape(b, T, MLA_HEADS, MLA_QK)
    q_m = jnp.concatenate([q_m[..., :MLA_NOPE], rope(q_m[..., MLA_NOPE:], pos)], axis=-1)
    q_m = rms_norm(q_m, q_norm_g)
    o_m = chunk_causal_attention(q_m, k_m, v_m, pos, k_pos)
    y_b = o_m.reshape(b, T, MLA_WIDTH) * jax.nn.silu(g_m)

    k_s = k_s.reshape(b, T, SB_HEADS, SB_HEAD_DIM)
    v_s = v_s.reshape(b, T, SB_HEADS, SB_HEAD_DIM)
    k_all = jnp.concatenate([sb_k_past.astype(k_s.dtype), k_s], axis=1)
    v_all = jnp.concatenate([sb_v_past.astype(v_s.dtype), v_s], axis=1)
    o_s = stick_breaking_attention(q_s.reshape(b, T, SB_HEADS, SB_HEAD_DIM), k_all, v_all, pos, k_pos)
    y_c = o_s.reshape(b, T, SB_WIDTH) * jax.nn.silu(g_s)

    gates = jax.nn.sigmoid(gate_logit.reshape(b, T, N_BRANCH, D_MODEL).astype(jnp.float32)
                           + b_merge.astype(jnp.float32)).astype(h.dtype)
    mixed = gates[:, :, 0] * (y_a @ w_branch[0])
    mixed = mixed + gates[:, :, 1] * (y_b @ w_branch[1])
    mixed = mixed + gates[:, :, 2] * (y_c @ w_branch[2])
    h = h + mixed @ w_out
    return h, (k_s, v_s, ckv, krope, ssd_new, conv_new)


def setup_inputs(seed: int = 0) -> dict:
    key = jax.random.key(seed)
    ks = jax.random.split(key, 24)
    f32 = jnp.float32

    def nrm(k, shape, scale):
        return scale * jax.random.normal(k, shape, f32)

    dt0 = jnp.exp(jax.random.uniform(ks[12], (DEPTH, SSD_HEADS), f32, math.log(1e-3), math.log(1e-1)))
    return {
        "x_prompt": nrm(ks[0], (BATCH, SEQ, D_MODEL), 1.0),
        "x_sample": nrm(ks[1], (DEC_BATCH, DEC_SEQ, D_MODEL), 1.0),
        "cache_sb_k": nrm(ks[2], (DEPTH, DEC_BATCH, PAST_LEN, SB_HEADS, SB_HEAD_DIM), 1.0),
        "cache_sb_v": nrm(ks[3], (DEPTH, DEC_BATCH, PAST_LEN, SB_HEADS, SB_HEAD_DIM), 1.0),
        "cache_mla_ckv": nrm(ks[4], (DEPTH, DEC_BATCH, PAST_LEN, KV_LORA), 1.0),
        "cache_mla_krope": nrm(ks[5], (DEPTH, DEC_BATCH, PAST_LEN, MLA_ROPE), 1.0),
        "state_ssd": nrm(ks[6], (DEPTH, DEC_BATCH, SSD_HEADS, SSD_HEAD_DIM, SSD_STATE), 0.1),
        "state_conv": nrm(ks[7], (DEPTH, DEC_BATCH, CONV_W - 1, CONV_CH), 1.0),
        "norm_g": 1.0 + nrm(ks[8], (DEPTH, D_MODEL), 0.02),
        "w_in": nrm(ks[9], (DEPTH, D_MODEL, IN_TOTAL), D_MODEL ** -0.5),
        "conv_w": nrm(ks[10], (DEPTH, CONV_W, CONV_CH), CONV_W ** -0.5),
        "conv_b": nrm(ks[11], (DEPTH, CONV_CH), 0.02),
        "dt_bias": dt0 + jnp.log(-jnp.expm1(-dt0)),
        "a_log": jnp.log(jax.random.uniform(ks[13], (DEPTH, SSD_HEADS), f32, 1.0, 16.0)),
        "d_skip": 1.0 + nrm(ks[14], (DEPTH, SSD_HEADS), 0.1),
        "ssd_norm_g": 1.0 + nrm(ks[15], (DEPTH, SSD_WIDTH), 0.02),
        "mla_kv_norm_g": 1.0 + nrm(ks[16], (DEPTH, KV_LORA), 0.02),
        "w_ukv": nrm(ks[17], (DEPTH, KV_LORA, MLA_HEADS * (MLA_NOPE + MLA_V)), KV_LORA ** -0.5),
        "mla_q_norm_g": 1.0 + nrm(ks[18], (DEPTH, MLA_QK), 0.02),
        "mla_k_norm_g": 1.0 + nrm(ks[19], (DEPTH, MLA_QK), 0.02),
        "w_branch": nrm(ks[20], (DEPTH, N_BRANCH, BRANCH_WIDTH, D_MODEL), BRANCH_WIDTH ** -0.5),
        "b_merge": nrm(ks[21], (DEPTH, N_BRANCH, D_MODEL), 0.02),
        "w_out": nrm(ks[22], (DEPTH, D_MODEL, D_MODEL), D_MODEL ** -0.5),
    }


def reference(x_prompt, x_sample, cache_sb_k, cache_sb_v, cache_mla_ckv, cache_mla_krope, state_ssd, state_conv,
              norm_g, w_in, conv_w, conv_b, dt_bias, a_log, d_skip, ssd_norm_g, mla_kv_norm_g, w_ukv,
              mla_q_norm_g, mla_k_norm_g, w_branch, b_merge, w_out):
    bp = x_prompt.shape[0]
    dtp = x_prompt.dtype
    empty_k = jnp.zeros((bp, 0, SB_HEADS, SB_HEAD_DIM), dtp)
    empty_ckv = jnp.zeros((bp, 0, KV_LORA), dtp)
    empty_kr = jnp.zeros((bp, 0, MLA_ROPE), dtp)
    zero_ssd = jnp.zeros((bp, SSD_HEADS, SSD_HEAD_DIM, SSD_STATE), jnp.float32)
    zero_conv = jnp.zeros((bp, CONV_W - 1, CONV_CH), dtp)

    hp, hs = x_prompt, x_sample
    outs_p, outs_s = [], []
    for l in range(DEPTH):
        p = (norm_g[l], w_in[l], conv_w[l], conv_b[l], dt_bias[l], a_log[l], d_skip[l], ssd_norm_g[l],
             mla_kv_norm_g[l], w_ukv[l], mla_q_norm_g[l], mla_k_norm_g[l], w_branch[l], b_merge[l], w_out[l])
        hp, st_p = trunk_layer(hp, empty_k, empty_k, empty_ckv, empty_kr, zero_ssd, zero_conv, *p)
        outs_p.append(st_p)
        hs, st_s = trunk_layer(hs, cache_sb_k[l], cache_sb_v[l], cache_mla_ckv[l], cache_mla_krope[l],
                               state_ssd[l], state_conv[l], *p)
        outs_s.append(st_s)

    sb_k_p, sb_v_p, ckv_p, krope_p, ssd_p, conv_p = [jnp.stack(s) for s in zip(*outs_p)]
    sb_k_s, sb_v_s, ckv_s, krope_s, ssd_s, conv_s = [jnp.stack(s) for s in zip(*outs_s)]
    return (hp, hs, sb_k_p, sb_v_p, ckv_p, krope_p, ssd_p, conv_p,
            sb_k_s, sb_v_s, ckv_s, krope_s, ssd_s, conv_s)
```

```python
import functools
import math

import jax
import jax.numpy as jnp
from jax import lax
from jax.experimental import pallas as pl
from jax.experimental.pallas import tpu as pltpu

CHUNK = 64
Q_BLOCK = 128
EPS = 1e-6
ROPE_THETA = 10000.0
LANES = 128
NEG_BIG = -1e30
VMEM_LIMIT_BYTES = 48 * 1024 * 1024

F32 = jnp.float32
BF16 = jnp.bfloat16


def _params(*sem):
    return pltpu.CompilerParams(dimension_semantics=sem, vmem_limit_bytes=VMEM_LIMIT_BYTES)


def _pick(n, prefs):
    for p in prefs:
        if n % p == 0:
            return p
    return n


def _sigmoid(x):
    return 1.0 / (1.0 + jnp.exp(-x))


def _silu(x):
    return x * _sigmoid(x)


def _softplus(x):
    return jnp.maximum(x, 0.0) + jnp.log(1.0 + jnp.exp(-jnp.abs(x)))


def _rmsnorm_kernel(x_ref, g_ref, o_ref):
    x = x_ref[...]
    ms = jnp.mean(x * x, axis=-1, keepdims=True)
    o_ref[...] = (x * lax.rsqrt(ms + EPS) * g_ref[...]).astype(o_ref.dtype)


def _rmsnorm(x2d, g):
    m, d = x2d.shape
    tm = _pick(m, (512, 256, 128, 64, 32, 16))
    return pl.pallas_call(
        _rmsnorm_kernel,
        grid=(m // tm,),
        in_specs=[pl.BlockSpec((tm, d), lambda i: (i, 0)), pl.BlockSpec((1, d), lambda i: (0, 0))],
        out_specs=pl.BlockSpec((tm, d), lambda i: (i, 0)),
        out_shape=jax.ShapeDtypeStruct((m, d), BF16),
        compiler_params=_params("parallel"),
        name="rmsnorm",
    )(x2d, g.reshape(1, d))


def _mm_kernel(x_ref, w_ref, o_ref):
    o_ref[...] = jnp.dot(x_ref[...], w_ref[...], preferred_element_type=F32).astype(o_ref.dtype)


def _mm_res_kernel(x_ref, w_ref, r_ref, o_ref):
    o_ref[...] = r_ref[...] + jnp.dot(x_ref[...], w_ref[...], preferred_element_type=F32)


def _matmul(x, w, residual=None, name="matmul"):
    m, k = x.shape
    n = w.shape[1]
    tm = _pick(m, (1024, 512, 256, 128, 64, 32, 16))
    tn = _pick(n, (1024, 512, 256, 128))
    in_specs = [pl.BlockSpec((tm, k), lambda i, j: (i, 0)), pl.BlockSpec((k, tn), lambda i, j: (0, j))]
    args = [x, w]
    kern = _mm_kernel
    if residual is not None:
        in_specs.append(pl.BlockSpec((tm, tn), lambda i, j: (i, j)))
        args.append(residual)
        kern = _mm_res_kernel
    return pl.pallas_call(
        kern,
        grid=(m // tm, n // tn),
        in_specs=in_specs,
        out_specs=pl.BlockSpec((tm, tn), lambda i, j: (i, j)),
        out_shape=jax.ShapeDtypeStruct((m, n), F32),
        compiler_params=_params("parallel", "arbitrary"),
        name=name,
    )(*args)


def _ssd_kernel(z_ref, xbc_ref, dt_ref, ctx_ref, h0_ref, cw_ref, cb_ref, dtb_ref, alog_ref, dskip_ref, ng_ref,
                y_ref, st_ref, pad_scr, y_scr, *, L, H, P, G, N, KW):
    c = pl.program_id(1)
    W = H * P
    hg = H // G
    gw = hg * P
    top = 8 - (KW - 1)

    @pl.when(c == 0)
    def _():
        pad_scr[top:8, :] = ctx_ref[0]
        st_ref[0] = h0_ref[0]

    pad_scr[8:8 + L, :] = xbc_ref[0]
    conv = cb_ref[...] + cw_ref[0:1, :] * pad_scr[top:top + L, :]
    for j in range(1, KW):
        conv = conv + cw_ref[j:j + 1, :] * pad_scr[top + j:top + j + L, :]
    pad_scr[top:8, :] = pad_scr[8 + L - (KW - 1):8 + L, :]
    conv = _silu(conv)
    xs = conv[:, :W]
    bm = conv[:, W:W + G * N]
    cm = conv[:, W + G * N:]

    dt = _softplus(dt_ref[0] + dtb_ref[...])
    a = -jnp.exp(alog_ref[...])
    da = dt * a
    row = lax.broadcasted_iota(jnp.int32, (L, L), 0)
    col = lax.broadcasted_iota(jnp.int32, (L, L), 1)
    tril = col <= row
    cum = jnp.dot(tril.astype(F32), da, precision=lax.Precision.HIGHEST, preferred_element_type=F32)
    cum_t = cum.T
    dt_t = dt.T
    cum_last = cum[L - 1:L, :]
    wend = jnp.exp(cum_last - cum) * dt
    ecum = jnp.exp(cum)
    cdec = jnp.exp(cum_last)
    left = lax.broadcasted_iota(jnp.int32, (L, LANES), 1) < P
    left1 = lax.broadcasted_iota(jnp.int32, (1, LANES), 1) < P

    for g in range(G):
        bg = bm[:, g * N:(g + 1) * N].astype(BF16)
        cg = cm[:, g * N:(g + 1) * N].astype(BF16)
        cbm = lax.dot_general(cg, bg, (((1,), (1,)), ((), ())), preferred_element_type=F32)
        st = st_ref[0, g]
        yoff = jnp.dot(cg, st.astype(BF16), preferred_element_type=F32)
        xw_parts, cd_parts = [], []
        for j in range(hg // 2):
            h1 = g * hg + 2 * j
            h2 = h1 + 1
            xp = xs[:, h1 * P:h1 * P + LANES]
            yp = None
            for h, keep_left in ((h1, True), (h2, False)):
                diff = cum[:, h:h + 1] - cum_t[h:h + 1, :]
                w = cbm * jnp.exp(jnp.where(tril, diff, NEG_BIG)) * dt_t[h:h + 1, :]
                xh = jnp.where(left, xp, 0.0) if keep_left else jnp.where(left, 0.0, xp)
                t = jnp.dot(w.astype(BF16), xh.astype(BF16), preferred_element_type=F32)
                yp = t if yp is None else yp + t
            e = jnp.where(left, ecum[:, h1:h1 + 1], ecum[:, h2:h2 + 1])
            y_scr[:, h1 * P:h1 * P + LANES] = yp + e * yoff[:, j * LANES:(j + 1) * LANES]
            we = jnp.where(left, wend[:, h1:h1 + 1], wend[:, h2:h2 + 1])
            xw_parts.append(xp * we)
            cd_parts.append(jnp.where(left1, cdec[:, h1:h1 + 1], cdec[:, h2:h2 + 1]))
        xw = jnp.concatenate(xw_parts, axis=1).astype(BF16)
        cd = jnp.concatenate(cd_parts, axis=1)
        upd = lax.dot_general(bg, xw, (((0,), (0,)), ((), ())), preferred_element_type=F32)
        st_ref[0, g] = cd * st + upd

    y = (y_scr[...] + dskip_ref[...] * xs) * _silu(z_ref[0])
    for g in range(G):
        yg = y[:, g * gw:(g + 1) * gw]
        ms = jnp.mean(yg * yg, axis=-1, keepdims=True)
        y_ref[0, :, g * gw:(g + 1) * gw] = (
            yg * lax.rsqrt(ms + EPS) * ng_ref[:, g * gw:(g + 1) * gw]).astype(y_ref.dtype)


def _ssd_branch(z, xbc, misc, dt_block, ctx, h0_t, cw, cb, dtb, alog, dskip_x, ng, *, H, P, G, N):
    b, T, W = z.shape
    C = xbc.shape[-1]
    KW = cw.shape[0]
    L = min(CHUNK, T)
    hg = H // G
    gw = hg * P
    assert T % L == 0 and L >= KW - 1 and hg % 2 == 0 and 2 * P == LANES and H <= LANES
    kern = functools.partial(_ssd_kernel, L=L, H=H, P=P, G=G, N=N, KW=KW)
    const2 = lambda bi, c: (0, 0)
    return pl.pallas_call(
        kern,
        grid=(b, T // L),
        in_specs=[
            pl.BlockSpec((1, L, W), lambda bi, c: (bi, c, 0)),
            pl.BlockSpec((1, L, C), lambda bi, c: (bi, c, 0)),
            pl.BlockSpec((1, L, LANES), lambda bi, c: (bi, c, dt_block)),
            pl.BlockSpec((1, KW - 1, C), lambda bi, c: (bi, 0, 0)),
            pl.BlockSpec((1, G, N, gw), lambda bi, c: (bi, 0, 0, 0)),
            pl.BlockSpec((KW, C), const2),
            pl.BlockSpec((1, C), const2),
            pl.BlockSpec((1, LANES), const2),
            pl.BlockSpec((1, LANES), const2),
            pl.BlockSpec((1, W), const2),
            pl.BlockSpec((1, W), const2),
        ],
        out_specs=[
            pl.BlockSpec((1, L, W), lambda bi, c: (bi, c, 0)),
            pl.BlockSpec((1, G, N, gw), lambda bi, c: (bi, 0, 0, 0)),
        ],
        out_shape=[jax.ShapeDtypeStruct((b, T, W), BF16), jax.ShapeDtypeStruct((b, G, N, gw), F32)],
        scratch_shapes=[pltpu.VMEM((8 + L, C), F32), pltpu.VMEM((L, W), F32)],
        compiler_params=_params("parallel", "arbitrary"),
        name="ssd_scan",
    )(z, xbc, misc, ctx, h0_t, cw, cb, dtb, alog, dskip_x, ng)


def _swap_halves(x, half):
    lane = lax.broadcasted_iota(jnp.int32, x.shape, 1)
    first = (lane % (2 * half)) < half
    return jnp.where(first, pltpu.roll(x, LANES - half, 1), pltpu.roll(x, half, 1))


def _mla_new_kernel(ckv_ref, kr_ref, cos_ref, sin_ref, g_ref, ckv_out, kr_out, *, R):
    x = ckv_ref[0]
    ms = jnp.mean(x * x, axis=-1, keepdims=True)
    ckv_out[0] = x * lax.rsqrt(ms + EPS) * g_ref[...]
    kr = kr_ref[0]
    rot = kr * cos_ref[...] + _swap_halves(kr, R // 2) * sin_ref[...]
    kr_out[0] = rot[:, :R]


def _mla_new(misc, cos2, sin2, kv_norm_g, *, KVL, R, kr_block):
    b, T, _ = misc.shape
    tb = _pick(T, (256, 128, 64))
    return pl.pallas_call(
        functools.partial(_mla_new_kernel, R=R),
        grid=(b, T // tb),
        in_specs=[
            pl.BlockSpec((1, tb, KVL), lambda bi, t: (bi, t, 0)),
            pl.BlockSpec((1, tb, LANES), lambda bi, t: (bi, t, kr_block)),
            pl.BlockSpec((tb, LANES), lambda bi, t: (t, 0)),
            pl.BlockSpec((tb, LANES), lambda bi, t: (t, 0)),
            pl.BlockSpec((1, KVL), lambda bi, t: (0, 0)),
        ],
        out_specs=[
            pl.BlockSpec((1, tb, KVL), lambda bi, t: (bi, t, 0)),
            pl.BlockSpec((1, tb, R), lambda bi, t: (bi, t, 0)),
        ],
        out_shape=[jax.ShapeDtypeStruct((b, T, KVL), F32), jax.ShapeDtypeStruct((b, T, R), F32)],
        compiler_params=_params("parallel", "parallel"),
        name="mla_new_kv",
    )(misc, misc, cos2, sin2, kv_norm_g.reshape(1, KVL))


def _mla_q_kernel(qn_ref, qr_ref, cos_ref, sin_ref, gn_ref, gr_ref, o_ref, *, HM, NOPE, R, scale):
    lane = lax.broadcasted_iota(jnp.int32, (qn_ref.shape[1], LANES), 1)
    left = lane < R
    for j in range(HM // 2):
        qr = qr_ref[0, :, j * LANES:(j + 1) * LANES]
        rot = qr * cos_ref[...] + _swap_halves(qr, R // 2) * sin_ref[...]
        sq = rot * rot
        for h, keep_left in ((2 * j, True), (2 * j + 1, False)):
            qn = qn_ref[0, :, h * NOPE:(h + 1) * NOPE]
            sel = left if keep_left else jnp.logical_not(left)
            ss = jnp.sum(qn * qn, axis=-1, keepdims=True) + jnp.sum(jnp.where(sel, sq, 0.0), axis=-1, keepdims=True)
            inv = lax.rsqrt(ss * (1.0 / (NOPE + R)) + EPS) * scale
            o_ref[0, :, h * 2 * LANES:h * 2 * LANES + NOPE] = (qn * inv * gn_ref[...]).astype(o_ref.dtype)
            o_ref[0, :, h * 2 * LANES + NOPE:(h + 1) * 2 * LANES] = (
                jnp.where(sel, rot * inv * gr_ref[...], 0.0)).astype(o_ref.dtype)


def _mla_q(pm, cos2, sin2, q_norm_g, *, HM, NOPE, R):
    b, T, _ = pm.shape
    assert NOPE == LANES and 2 * R == LANES and HM % 2 == 0
    tb = _pick(T, (256, 128, 64))
    scale = 1.0 / math.sqrt(NOPE + R)
    gn = q_norm_g[:NOPE].reshape(1, NOPE)
    gr = jnp.tile(q_norm_g[NOPE:], 2).reshape(1, LANES)
    nope_w, rope_w = HM * NOPE, HM * R
    assert nope_w % rope_w == 0
    return pl.pallas_call(
        functools.partial(_mla_q_kernel, HM=HM, NOPE=NOPE, R=R, scale=scale),
        grid=(b, T // tb),
        in_specs=[
            pl.BlockSpec((1, tb, nope_w), lambda bi, t: (bi, t, 0)),
            pl.BlockSpec((1, tb, rope_w), lambda bi, t: (bi, t, nope_w // rope_w)),
            pl.BlockSpec((tb, LANES), lambda bi, t: (t, 0)),
            pl.BlockSpec((tb, LANES), lambda bi, t: (t, 0)),
            pl.BlockSpec((1, NOPE), lambda bi, t: (0, 0)),
            pl.BlockSpec((1, LANES), lambda bi, t: (0, 0)),
        ],
        out_specs=pl.BlockSpec((1, tb, HM * 2 * LANES), lambda bi, t: (bi, t, 0)),
        out_shape=jax.ShapeDtypeStruct((b, T, HM * 2 * LANES), BF16),
        compiler_params=_params("parallel", "parallel"),
        name="mla_q_prep",
    )(pm, pm, cos2, sin2, gn, gr)


def _mla_kv_kernel(ckv_ref, kr_ref, w_ref, gn_ref, gr_ref, k_ref, v_ref, *, HM, NOPE, R, V):
    ckv = ckv_ref[0].astype(BF16)
    kv = jnp.dot(ckv, w_ref[...], preferred_element_type=F32)
    kr = kr_ref[0]
    ss_r = jnp.sum(kr * kr, axis=-1, keepdims=True)
    krg = kr * gr_ref[...]
    zero = jnp.zeros_like(krg)
    kr_even = jnp.concatenate([krg, zero], axis=1)
    kr_odd = jnp.concatenate([zero, krg], axis=1)
    for h in range(HM):
        base = h * (NOPE + V)
        kn = kv[:, base:base + NOPE]
        ss = jnp.sum(kn * kn, axis=-1, keepdims=True) + ss_r
        inv = lax.rsqrt(ss * (1.0 / (NOPE + R)) + EPS)
        k_ref[0, :, h * 2 * LANES:h * 2 * LANES + NOPE] = (kn * inv * gn_ref[...]).astype(k_ref.dtype)
        k_ref[0, :, h * 2 * LANES + NOPE:(h + 1) * 2 * LANES] = (
            (kr_even if h % 2 == 0 else kr_odd) * inv).astype(k_ref.dtype)
        v_ref[0, :, h * V:(h + 1) * V] = kv[:, base + NOPE:base + NOPE + V].astype(v_ref.dtype)


def _mla_kv(ckv_all, krope_all, w_ukv, k_norm_g, *, HM, NOPE, R, V):
    b, S, KVL = ckv_all.shape
    ts = _pick(S, (256, 320, 128, 64))
    gn = k_norm_g[:NOPE].reshape(1, NOPE)
    gr = k_norm_g[NOPE:].reshape(1, R)
    return pl.pallas_call(
        functools.partial(_mla_kv_kernel, HM=HM, NOPE=NOPE, R=R, V=V),
        grid=(b, S // ts),
        in_specs=[
            pl.BlockSpec((1, ts, KVL), lambda bi, s: (bi, s, 0)),
            pl.BlockSpec((1, ts, R), lambda bi, s: (bi, s, 0)),
            pl.BlockSpec((KVL, HM * (NOPE + V)), lambda bi, s: (0, 0)),
            pl.BlockSpec((1, NOPE), lambda bi, s: (0, 0)),
            pl.BlockSpec((1, R), lambda bi, s: (0, 0)),
        ],
        out_specs=[
            pl.BlockSpec((1, ts, HM * 2 * LANES), lambda bi, s: (bi, s, 0)),
            pl.BlockSpec((1, ts, HM * V), lambda bi, s: (bi, s, 0)),
        ],
        out_shape=[jax.ShapeDtypeStruct((b, S, HM * 2 * LANES), BF16), jax.ShapeDtypeStruct((b, S, HM * V), BF16)],
        compiler_params=_params("parallel", "parallel"),
        name="mla_kv_prep",
    )(ckv_all, krope_all, w_ukv, gn, gr)


def _mla_attn_kernel(q_ref, k_ref, v_ref, g_ref, o_ref, *, past, qb, kb, shift):
    qi = pl.program_id(2)
    q = q_ref[0]
    nkv = (past + (qi + 1) * qb) // kb
    qchunk = lax.shift_right_logical(past + qi * qb + lax.broadcasted_iota(jnp.int32, (qb, 1), 0), shift)
    kiota = lax.broadcasted_iota(jnp.int32, (1, kb), 1)

    def body(j, carry):
        m, l, acc = carry
        start = pl.multiple_of(j * kb, kb)
        k = k_ref[0, pl.ds(start, kb), :]
        v = v_ref[0, pl.ds(start, kb), :]
        s = lax.dot_general(q, k, (((1,), (1,)), ((), ())), preferred_element_type=F32)
        kchunk = lax.shift_right_logical(j * kb + kiota, shift)
        s = jnp.where(kchunk <= qchunk, s, NEG_BIG)
        m_new = jnp.maximum(m, jnp.max(s, axis=-1, keepdims=True))
        p = jnp.exp(s - m_new)
        alpha = jnp.exp(m - m_new)
        l = alpha * l + jnp.sum(p, axis=-1, keepdims=True)
        acc = alpha * acc + jnp.dot(p.astype(BF16), v, preferred_element_type=F32)
        return m_new, l, acc

    dv = v_ref.shape[-1]
    init = (jnp.full((qb, 1), NEG_BIG, F32), jnp.zeros((qb, 1), F32), jnp.zeros((qb, dv), F32))
    _, l, acc = lax.fori_loop(0, nkv, body, init)
    o_ref[0] = (acc / l * _silu(g_ref[0])).astype(o_ref.dtype)


def _mla_attn(q_full, k_full, v, pm, *, HM, V, past, g_block0):
    b, T, _ = q_full.shape
    S = k_full.shape[1]
    assert CHUNK & (CHUNK - 1) == 0 and V == LANES
    if past == 0:
        qb = _pick(T, (256, 128, 64))
        kb = qb
    else:
        qb = min(Q_BLOCK, T)
        assert T == qb, "running streams advance by at most one query block"
        kb = _pick(S, (320, 256, 128, 64))
    assert (past + qb) % kb == 0
    kern = functools.partial(_mla_attn_kernel, past=past, qb=qb, kb=kb, shift=CHUNK.bit_length() - 1)
    return pl.pallas_call(
        kern,
        grid=(b, HM, T // qb),
        in_specs=[
            pl.BlockSpec((1, qb, 2 * LANES), lambda bi, h, t: (bi, t, h)),
            pl.BlockSpec((1, S, 2 * LANES), lambda bi, h, t: (bi, 0, h)),
            pl.BlockSpec((1, S, V), lambda bi, h, t: (bi, 0, h)),
            pl.BlockSpec((1, qb, V), lambda bi, h, t: (bi, t, g_block0 + h)),
        ],
        out_specs=pl.BlockSpec((1, qb, V), lambda bi, h, t: (bi, t, h)),
        out_shape=jax.ShapeDtypeStruct((b, T, HM * V), BF16),
        compiler_params=_params("parallel", "parallel", "arbitrary"),
        name="mla_attention",
    )(q_full, k_full, v, pm)


def _sb_attn_kernel(q_ref, k_ref, v_ref, g_ref, o_ref, *, past, qb, kb, scale):
    qi = pl.program_id(2)
    q = q_ref[0].astype(BF16)
    nblk = (past + (qi + 1) * qb) // kb
    qpos = past + qi * qb + lax.broadcasted_iota(jnp.int32, (qb, 1), 0)
    kiota = lax.broadcasted_iota(jnp.int32, (1, kb), 1)
    tri = (lax.broadcasted_iota(jnp.int32, (kb, kb), 0) > lax.broadcasted_iota(jnp.int32, (kb, kb), 1)).astype(BF16)

    def body(i, carry):
        run, acc = carry
        j = nblk - 1 - i
        start = pl.multiple_of(j * kb, kb)
        k = k_ref[0, pl.ds(start, kb), :].astype(BF16)
        v = v_ref[0, pl.ds(start, kb), :].astype(BF16)
        z = lax.dot_general(q, k, (((1,), (1,)), ((), ())), preferred_element_type=F32) * scale
        mask = (j * kb + kiota) < qpos
        sp = _softplus(z)
        l1m = jnp.where(mask, -sp, 0.0)
        hi = l1m.astype(BF16)
        lo = (l1m - hi.astype(F32)).astype(BF16)
        after = (jnp.dot(hi, tri, preferred_element_type=F32) + jnp.dot(lo, tri, preferred_element_type=F32)) + run
        att = jnp.where(mask, jnp.exp(z - sp + after), 0.0)
        acc = acc + jnp.dot(att.astype(BF16), v, preferred_element_type=F32)
        run = run + jnp.sum(l1m, axis=-1, keepdims=True)
        return run, acc

    d = v_ref.shape[-1]
    _, acc = lax.fori_loop(0, nblk, body, (jnp.zeros((qb, 1), F32), jnp.zeros((qb, d), F32)))
    o_ref[0] = (acc * _silu(g_ref[0])).astype(o_ref.dtype)


def _sb_attn(psb, k_all, v_all, *, HS, DS, past):
    b, T, _ = psb.shape
    S = k_all.shape[1]
    assert DS == LANES
    qb = min(Q_BLOCK, T)
    kb = qb if past % qb == 0 else math.gcd(past, qb)
    assert T % qb == 0 and past % kb == 0 and qb % kb == 0
    kern = functools.partial(_sb_attn_kernel, past=past, qb=qb, kb=kb, scale=1.0 / math.sqrt(DS))
    return pl.pallas_call(
        kern,
        grid=(b, HS, T // qb),
        in_specs=[
            pl.BlockSpec((1, qb, DS), lambda bi, h, t: (bi, t, h)),
            pl.BlockSpec((1, S, DS), lambda bi, h, t: (bi, 0, h)),
            pl.BlockSpec((1, S, DS), lambda bi, h, t: (bi, 0, h)),
            pl.BlockSpec((1, qb, DS), lambda bi, h, t: (bi, t, HS + h)),
        ],
        out_specs=pl.BlockSpec((1, qb, DS), lambda bi, h, t: (bi, t, h)),
        out_shape=jax.ShapeDtypeStruct((b, T, HS * DS), BF16),
        compiler_params=_params("parallel", "parallel", "arbitrary"),
        name="sb_attention",
    )(psb, k_all, v_all, psb)


def _merge_kernel(ya_ref, yb_ref, yc_ref, w_ref, g0_ref, g1_ref, g2_ref, bm_ref, o_ref):
    acc = None
    for i, (y_ref, g_ref) in enumerate(((ya_ref, g0_ref), (yb_ref, g1_ref), (yc_ref, g2_ref))):
        gate = _sigmoid(g_ref[...] + bm_ref[i])
        t = gate * jnp.dot(y_ref[...], w_ref[i], preferred_element_type=F32)
        acc = t if acc is None else acc + t
    o_ref[...] = acc.astype(o_ref.dtype)


def _merge(ya, yb, yc, w_branch, gate_logit, b_merge):
    m, k = ya.shape
    nb, _, d = w_branch.shape
    tm = _pick(m, (512, 256, 128, 64, 32, 16))
    tn = _pick(d, (512, 256, 128))
    nt = d // tn
    yspec = pl.BlockSpec((tm, k), lambda i, j: (i, 0))
    gspecs = [pl.BlockSpec((tm, tn), functools.partial(lambda i, j, br: (i, br * nt + j), br=br)) for br in range(nb)]
    return pl.pallas_call(
        _merge_kernel,
        grid=(m // tm, nt),
        in_specs=[yspec, yspec, yspec, pl.BlockSpec((nb, k, tn), lambda i, j: (0, 0, j))] + gspecs
        + [pl.BlockSpec((nb, 1, tn), lambda i, j: (0, 0, j))],
        out_specs=pl.BlockSpec((tm, tn), lambda i, j: (i, j)),
        out_shape=jax.ShapeDtypeStruct((m, d), BF16),
        compiler_params=_params("parallel", "arbitrary"),
        name="branch_merge",
    )(ya, yb, yc, w_branch, gate_logit, gate_logit, gate_logit, b_merge.reshape(nb, 1, d))


def _rope_tables(past, T, R):
    half = R // 2
    freq = ROPE_THETA ** (-jnp.arange(half, dtype=F32) / half)
    ang = (past + jnp.arange(T)).astype(F32)[:, None] * freq[None, :]
    cos, sin = jnp.cos(ang), jnp.sin(ang)
    reps = LANES // R
    return jnp.tile(jnp.concatenate([cos, cos], axis=1), (1, reps)), jnp.tile(jnp.concatenate([-sin, sin], axis=1), (1, reps))


def _prep_layer(dims, norm_g, w_in, conv_w, conv_b, dt_bias, a_log, d_skip, ssd_norm_g, kv_norm_g, w_ukv,
                q_norm_g, k_norm_g, w_branch, b_merge, w_out):
    D = w_in.shape[0]
    W, C, H, P = dims["W"], dims["C"], dims["H"], dims["P"]
    HM, NOPE, R, KVL, HS, DS = dims["HM"], dims["NOPE"], dims["R"], dims["KVL"], dims["HS"], dims["DS"]
    sizes = (W, C, H, HM * (NOPE + R), KVL, R, HM * dims["V"], HS * DS, HS * DS, HS * DS, HS * DS, w_branch.shape[0] * D)
    offs = [0]
    for s in sizes:
        offs.append(offs[-1] + s)
    col = lambda i: w_in[:, offs[i]:offs[i + 1]]
    pad_to = lambda w, n: jnp.pad(w, ((0, 0), (0, n - w.shape[1])))
    wq = col(3).reshape(D, HM, NOPE + R)
    w_mla = jnp.concatenate([wq[:, :, :NOPE].reshape(D, HM * NOPE), wq[:, :, NOPE:].reshape(D, HM * R), col(6)], axis=1)
    w_misc = jnp.concatenate([col(4), pad_to(col(2), LANES), pad_to(col(5), LANES)], axis=1)
    vec = lambda v: jnp.pad(v, (0, LANES - v.shape[0])).reshape(1, LANES)
    return dict(
        norm_g=norm_g,
        w_z=col(0).astype(BF16), w_xbc=col(1).astype(BF16), w_misc=w_misc.astype(BF16), w_mla=w_mla.astype(BF16),
        w_sb=jnp.concatenate([col(7), col(10)], axis=1).astype(BF16),
        w_k=col(8).astype(BF16), w_v=col(9).astype(BF16), w_gate=col(11).astype(BF16),
        conv_w=conv_w, conv_b=conv_b.reshape(1, C), dt_bias=vec(dt_bias), a_log=vec(a_log),
        dskip_x=jnp.repeat(d_skip, P).reshape(1, W), ssd_norm_g=ssd_norm_g.reshape(1, W),
        kv_norm_g=kv_norm_g, w_ukv=w_ukv.astype(BF16), q_norm_g=q_norm_g, k_norm_g=k_norm_g,
        w_branch=w_branch.astype(BF16), b_merge=b_merge, w_out=w_out.astype(BF16),
    )


def _layer(dims, p, h, sb_k_past, sb_v_past, ckv_past, krope_past, ssd_h0, conv_ctx):
    b, T, D = h.shape
    M = b * T
    H, P, G, N = dims["H"], dims["P"], dims["G"], dims["N"]
    HM, NOPE, R, V, KVL, HS, DS = dims["HM"], dims["NOPE"], dims["R"], dims["V"], dims["KVL"], dims["HS"], dims["DS"]
    hg = H // G
    past = 0 if sb_k_past is None else sb_k_past.shape[1]

    u = _rmsnorm(h.reshape(M, D), p["norm_g"])
    z = _matmul(u, p["w_z"], name="proj_z").reshape(b, T, -1)
    xbc = _matmul(u, p["w_xbc"], name="proj_xbc").reshape(b, T, -1)
    misc = _matmul(u, p["w_misc"], name="proj_misc").reshape(b, T, -1)
    pm = _matmul(u, p["w_mla"], name="proj_mla").reshape(b, T, -1)
    psb = _matmul(u, p["w_sb"], name="proj_sb").reshape(b, T, -1)
    k_s = _matmul(u, p["w_k"], name="proj_sbk").reshape(b, T, -1)
    v_s = _matmul(u, p["w_v"], name="proj_sbv").reshape(b, T, -1)
    gate_logit = _matmul(u, p["w_gate"], name="proj_gate")

    KW = p["conv_w"].shape[0]
    assert T >= KW - 1
    if ssd_h0 is None:
        h0_t = jnp.zeros((b, G, N, hg * P), F32)
        ctx = jnp.zeros((b, KW - 1, xbc.shape[-1]), F32)
    else:
        h0_t = jnp.swapaxes(ssd_h0.reshape(b, G, hg * P, N), 2, 3)
        ctx = conv_ctx
    y_a, st_t = _ssd_branch(z, xbc, misc, KVL // LANES, ctx, h0_t, p["conv_w"], p["conv_b"], p["dt_bias"],
                            p["a_log"], p["dskip_x"], p["ssd_norm_g"], H=H, P=P, G=G, N=N)
    ssd_new = jnp.swapaxes(st_t, 2, 3).reshape(b, H, P, N)
    conv_new = xbc[:, T - (KW - 1):, :]

    cos2, sin2 = _rope_tables(past, T, R)
    ckv, krope = _mla_new(misc, cos2, sin2, p["kv_norm_g"], KVL=KVL, R=R, kr_block=KVL // LANES + 1)
    if past:
        ckv_all = jnp.concatenate([ckv_past, ckv], axis=1)
        krope_all = jnp.concatenate([krope_past, krope], axis=1)
    else:
        ckv_all, krope_all = ckv, krope
    q_full = _mla_q(pm, cos2, sin2, p["q_norm_g"], HM=HM, NOPE=NOPE, R=R)
    k_full, v_m = _mla_kv(ckv_all, krope_all, p["w_ukv"], p["k_norm_g"], HM=HM, NOPE=NOPE, R=R, V=V)
    y_b = _mla_attn(q_full, k_full, v_m, pm, HM=HM, V=V, past=past, g_block0=(HM * (NOPE + R)) // V)

    if past:
        k_all = jnp.concatenate([sb_k_past.reshape(b, past, HS * DS), k_s], axis=1)
        v_all = jnp.concatenate([sb_v_past.reshape(b, past, HS * DS), v_s], axis=1)
    else:
        k_all, v_all = k_s, v_s
    y_c = _sb_attn(psb, k_all, v_all, HS=HS, DS=DS, past=past)

    mixed = _merge(y_a.reshape(M, -1), y_b.reshape(M, -1), y_c.reshape(M, -1), p["w_branch"], gate_logit, p["b_merge"])
    h_new = _matmul(mixed, p["w_out"], residual=h.reshape(M, D), name="out_proj").reshape(b, T, D)
    state = (k_s.reshape(b, T, HS, DS), v_s.reshape(b, T, HS, DS), ckv, krope, ssd_new, conv_new)
    return h_new, state


def kernel(x_prompt, x_sample, cache_sb_k, cache_sb_v, cache_mla_ckv, cache_mla_krope, state_ssd, state_conv,
           norm_g, w_in, conv_w, conv_b, dt_bias, a_log, d_skip, ssd_norm_g, mla_kv_norm_g, w_ukv,
           mla_q_norm_g, mla_k_norm_g, w_branch, b_merge, w_out):
    depth = w_in.shape[0]
    H, P, N = state_ssd.shape[2:]
    C = conv_w.shape[-1]
    W = H * P
    HS, DS = cache_sb_k.shape[3:]
    KVL = cache_mla_ckv.shape[-1]
    R = cache_mla_krope.shape[-1]
    NOPE = mla_q_norm_g.shape[-1] - R
    HM = (w_ukv.shape[-1] - w_branch.shape[2]) // NOPE
    V = w_ukv.shape[-1] // HM - NOPE
    dims = dict(H=H, P=P, N=N, C=C, W=W, G=(C - W) // (2 * N), HS=HS, DS=DS, KVL=KVL, R=R, NOPE=NOPE, HM=HM, V=V)
    assert KVL % LANES == 0 and HM * V == HS * DS == W

    hp, hs = x_prompt, x_sample
    outs_p, outs_s = [], []
    for l in range(depth):
        p = _prep_layer(dims, norm_g[l], w_in[l], conv_w[l], conv_b[l], dt_bias[l], a_log[l], d_skip[l], ssd_norm_g[l],
                        mla_kv_norm_g[l], w_ukv[l], mla_q_norm_g[l], mla_k_norm_g[l], w_branch[l], b_merge[l], w_out[l])
        hp, st_p = _layer(dims, p, hp, None, None, None, None, None, None)
        outs_p.append(st_p)
        hs, st_s = _layer(dims, p, hs, cache_sb_k[l], cache_sb_v[l], cache_mla_ckv[l], cache_mla_krope[l],
                          state_ssd[l], state_conv[l])
        outs_s.append(st_s)
    stack_p = [jnp.stack(s) for s in zip(*outs_p)]
    stack_s = [jnp.stack(s) for s in zip(*outs_s)]
    return (hp, hs, *stack_p, *stack_s)
```

```python
import functools
import math

import jax
import jax.numpy as jnp
from jax import lax
from jax.experimental import pallas as pl
from jax.experimental.pallas import tpu as pltpu

CHUNK = 64
Q_BLOCK = 128
EPS = 1e-6
ROPE_THETA = 10000.0
LANES = 128
NEG_BIG = -1e30
LOG2E = 1.4426950408889634
VMEM_LIMIT_BYTES = 48 * 1024 * 1024

F32 = jnp.float32
BF16 = jnp.bfloat16


def _params(*sem):
    return pltpu.CompilerParams(dimension_semantics=sem, vmem_limit_bytes=VMEM_LIMIT_BYTES)


def _pick(n, prefs):
    for p in prefs:
        if n % p == 0:
            return p
    return n


def _sigmoid(x):
    return 1.0 / (1.0 + jnp.exp(-x))


def _silu(x):
    return x * _sigmoid(x)


def _softplus(x):
    return jnp.maximum(x, 0.0) + jnp.log(1.0 + jnp.exp(-jnp.abs(x)))


def _rmsnorm_kernel(x_ref, g_ref, o_ref):
    x = x_ref[...]
    ms = jnp.mean(x * x, axis=-1, keepdims=True)
    o_ref[...] = (x * lax.rsqrt(ms + EPS) * g_ref[...]).astype(o_ref.dtype)


def _rmsnorm(x2d, g):
    m, d = x2d.shape
    tm = _pick(m, (512, 256, 128, 64, 32, 16))
    return pl.pallas_call(
        _rmsnorm_kernel,
        grid=(m // tm,),
        in_specs=[pl.BlockSpec((tm, d), lambda i: (i, 0)), pl.BlockSpec((1, d), lambda i: (0, 0))],
        out_specs=pl.BlockSpec((tm, d), lambda i: (i, 0)),
        out_shape=jax.ShapeDtypeStruct((m, d), BF16),
        compiler_params=_params("parallel"),
        name="rmsnorm",
    )(x2d, g.reshape(1, d))


def _mm_kernel(x_ref, w_ref, o_ref):
    o_ref[...] = jnp.dot(x_ref[...], w_ref[...], preferred_element_type=F32).astype(o_ref.dtype)


def _mm_res_kernel(x_ref, w_ref, r_ref, o_ref):
    o_ref[...] = r_ref[...] + jnp.dot(x_ref[...], w_ref[...], preferred_element_type=F32)


def _mm_into_kernel(x_ref, w_ref, prev_ref, o_ref):
    del prev_ref
    _mm_kernel(x_ref, w_ref, o_ref)


def _matmul(x, w, residual=None, name="matmul", stack=None):
    m, k = x.shape
    n = w.shape[1]
    tm = _pick(m, (1024, 512, 256, 128, 64, 32, 16))
    tn = _pick(n, (1024, 512, 256, 128))
    in_specs = [pl.BlockSpec((tm, k), lambda i, j: (i, 0)), pl.BlockSpec((k, tn), lambda i, j: (0, j))]
    args = [x, w]
    kern = _mm_kernel
    out_spec = pl.BlockSpec((tm, tn), lambda i, j: (i, j))
    out_shape = jax.ShapeDtypeStruct((m, n), F32)
    aliases = {}
    if residual is not None:
        in_specs.append(pl.BlockSpec((tm, tn), lambda i, j: (i, j)))
        args.append(residual)
        kern = _mm_res_kernel
    if stack is not None:
        prev, layer, depth = stack
        out_spec = pl.BlockSpec((None, tm, tn), lambda i, j: (layer, i, j))
        out_shape = jax.ShapeDtypeStruct((depth, m, n), F32)
        if prev is not None:
            in_specs.append(pl.BlockSpec(memory_space=pl.ANY))
            args.append(prev)
            kern = _mm_into_kernel
            aliases = {2: 0}
    return pl.pallas_call(
        kern,
        grid=(m // tm, n // tn),
        in_specs=in_specs,
        out_specs=out_spec,
        out_shape=out_shape,
        input_output_aliases=aliases,
        compiler_params=_params("parallel", "arbitrary"),
        name=name,
    )(*args)


def _ssd_kernel(z_ref, xbc_ref, dt_ref, ctx_ref, h0_ref, cw_ref, cb_ref, dtb_ref, alog_ref, dskip_ref, ng_ref,
                y_ref, st_ref, pad_scr, y_scr, *, L, H, P, G, N, KW):
    c = pl.program_id(1)
    W = H * P
    hg = H // G
    gw = hg * P
    top = 8 - (KW - 1)

    @pl.when(c == 0)
    def _():
        pad_scr[top:8, :] = ctx_ref[0]
        st_ref[0] = h0_ref[0]

    pad_scr[8:8 + L, :] = xbc_ref[0]
    conv = cb_ref[...] + cw_ref[0:1, :] * pad_scr[top:top + L, :]
    for j in range(1, KW):
        conv = conv + cw_ref[j:j + 1, :] * pad_scr[top + j:top + j + L, :]
    pad_scr[top:8, :] = pad_scr[8 + L - (KW - 1):8 + L, :]
    conv = _silu(conv)
    xs = conv[:, :W]
    bm = conv[:, W:W + G * N]
    cm = conv[:, W + G * N:]

    dt = _softplus(dt_ref[0] + dtb_ref[...])
    a = -jnp.exp(alog_ref[...])
    da = dt * a
    row = lax.broadcasted_iota(jnp.int32, (L, L), 0)
    col = lax.broadcasted_iota(jnp.int32, (L, L), 1)
    tril = col <= row
    cum = jnp.dot(tril.astype(F32), da, precision=lax.Precision.HIGHEST, preferred_element_type=F32)
    cum_t = cum.T
    dt_t = dt.T
    cum_last = cum[L - 1:L, :]
    wend = jnp.exp(cum_last - cum) * dt
    ecum = jnp.exp(cum)
    cdec = jnp.exp(cum_last)
    left = lax.broadcasted_iota(jnp.int32, (L, LANES), 1) < P
    left1 = lax.broadcasted_iota(jnp.int32, (1, LANES), 1) < P

    for g in range(G):
        bg = bm[:, g * N:(g + 1) * N].astype(BF16)
        cg = cm[:, g * N:(g + 1) * N].astype(BF16)
        cbm = lax.dot_general(cg, bg, (((1,), (1,)), ((), ())), preferred_element_type=F32)
        st = st_ref[0, g]
        yoff = jnp.dot(cg, st.astype(BF16), preferred_element_type=F32)
        xw_parts, cd_parts = [], []
        for j in range(hg // 2):
            h1 = g * hg + 2 * j
            h2 = h1 + 1
            xp = xs[:, h1 * P:h1 * P + LANES]
            yp = None
            for h, keep_left in ((h1, True), (h2, False)):
                diff = cum[:, h:h + 1] - cum_t[h:h + 1, :]
                w = cbm * jnp.exp(jnp.where(tril, diff, NEG_BIG)) * dt_t[h:h + 1, :]
                xh = jnp.where(left, xp, 0.0) if keep_left else jnp.where(left, 0.0, xp)
                t = jnp.dot(w.astype(BF16), xh.astype(BF16), preferred_element_type=F32)
                yp = t if yp is None else yp + t
            e = jnp.where(left, ecum[:, h1:h1 + 1], ecum[:, h2:h2 + 1])
            y_scr[:, h1 * P:h1 * P + LANES] = yp + e * yoff[:, j * LANES:(j + 1) * LANES]
            we = jnp.where(left, wend[:, h1:h1 + 1], wend[:, h2:h2 + 1])
            xw_parts.append(xp * we)
            cd_parts.append(jnp.where(left1, cdec[:, h1:h1 + 1], cdec[:, h2:h2 + 1]))
        xw = jnp.concatenate(xw_parts, axis=1).astype(BF16)
        cd = jnp.concatenate(cd_parts, axis=1)
        upd = lax.dot_general(bg, xw, (((0,), (0,)), ((), ())), preferred_element_type=F32)
        st_ref[0, g] = cd * st + upd

    y = (y_scr[...] + dskip_ref[...] * xs) * _silu(z_ref[0])
    for g in range(G):
        yg = y[:, g * gw:(g + 1) * gw]
        ms = jnp.mean(yg * yg, axis=-1, keepdims=True)
        y_ref[0, :, g * gw:(g + 1) * gw] = (
            yg * lax.rsqrt(ms + EPS) * ng_ref[:, g * gw:(g + 1) * gw]).astype(y_ref.dtype)


def _ssd_branch(z, xbc, misc, dt_block, ctx, h0_t, cw, cb, dtb, alog, dskip_x, ng, *, H, P, G, N):
    b, T, W = z.shape
    C = xbc.shape[-1]
    KW = cw.shape[0]
    L = min(CHUNK, T)
    hg = H // G
    gw = hg * P
    assert T % L == 0 and L >= KW - 1 and hg % 2 == 0 and 2 * P == LANES and H <= LANES
    kern = functools.partial(_ssd_kernel, L=L, H=H, P=P, G=G, N=N, KW=KW)
    const2 = lambda bi, c: (0, 0)
    return pl.pallas_call(
        kern,
        grid=(b, T // L),
        in_specs=[
            pl.BlockSpec((1, L, W), lambda bi, c: (bi, c, 0)),
            pl.BlockSpec((1, L, C), lambda bi, c: (bi, c, 0)),
            pl.BlockSpec((1, L, LANES), lambda bi, c: (bi, c, dt_block)),
            pl.BlockSpec((1, KW - 1, C), lambda bi, c: (bi, 0, 0)),
            pl.BlockSpec((1, G, N, gw), lambda bi, c: (bi, 0, 0, 0)),
            pl.BlockSpec((KW, C), const2),
            pl.BlockSpec((1, C), const2),
            pl.BlockSpec((1, LANES), const2),
            pl.BlockSpec((1, LANES), const2),
            pl.BlockSpec((1, W), const2),
            pl.BlockSpec((1, W), const2),
        ],
        out_specs=[
            pl.BlockSpec((1, L, W), lambda bi, c: (bi, c, 0)),
            pl.BlockSpec((1, G, N, gw), lambda bi, c: (bi, 0, 0, 0)),
        ],
        out_shape=[jax.ShapeDtypeStruct((b, T, W), BF16), jax.ShapeDtypeStruct((b, G, N, gw), F32)],
        scratch_shapes=[pltpu.VMEM((8 + L, C), F32), pltpu.VMEM((L, W), F32)],
        compiler_params=_params("parallel", "arbitrary"),
        name="ssd_scan",
    )(z, xbc, misc, ctx, h0_t, cw, cb, dtb, alog, dskip_x, ng)


def _swap_halves(x, half):
    lane = lax.broadcasted_iota(jnp.int32, x.shape, 1)
    first = (lane % (2 * half)) < half
    return jnp.where(first, pltpu.roll(x, LANES - half, 1), pltpu.roll(x, half, 1))


def _mla_new_kernel(ckv_ref, kr_ref, cos_ref, sin_ref, g_ref, ckv_out, kr_out, *, R):
    x = ckv_ref[0]
    ms = jnp.mean(x * x, axis=-1, keepdims=True)
    ckv_out[0] = x * lax.rsqrt(ms + EPS) * g_ref[...]
    kr = kr_ref[0]
    rot = kr * cos_ref[...] + _swap_halves(kr, R // 2) * sin_ref[...]
    kr_out[0] = rot[:, :R]


def _mla_new(misc, cos2, sin2, kv_norm_g, *, KVL, R, kr_block):
    b, T, _ = misc.shape
    tb = _pick(T, (256, 128, 64))
    return pl.pallas_call(
        functools.partial(_mla_new_kernel, R=R),
        grid=(b, T // tb),
        in_specs=[
            pl.BlockSpec((1, tb, KVL), lambda bi, t: (bi, t, 0)),
            pl.BlockSpec((1, tb, LANES), lambda bi, t: (bi, t, kr_block)),
            pl.BlockSpec((tb, LANES), lambda bi, t: (t, 0)),
            pl.BlockSpec((tb, LANES), lambda bi, t: (t, 0)),
            pl.BlockSpec((1, KVL), lambda bi, t: (0, 0)),
        ],
        out_specs=[
            pl.BlockSpec((1, tb, KVL), lambda bi, t: (bi, t, 0)),
            pl.BlockSpec((1, tb, R), lambda bi, t: (bi, t, 0)),
        ],
        out_shape=[jax.ShapeDtypeStruct((b, T, KVL), F32), jax.ShapeDtypeStruct((b, T, R), F32)],
        compiler_params=_params("parallel", "parallel"),
        name="mla_new_kv",
    )(misc, misc, cos2, sin2, kv_norm_g.reshape(1, KVL))


def _mla_q_kernel(qn_ref, qr_ref, cos_ref, sin_ref, gn_ref, gr_ref, o_ref, *, HM, NOPE, R, scale):
    lane = lax.broadcasted_iota(jnp.int32, (qn_ref.shape[1], LANES), 1)
    left = lane < R
    for j in range(HM // 2):
        qr = qr_ref[0, :, j * LANES:(j + 1) * LANES]
        rot = qr * cos_ref[...] + _swap_halves(qr, R // 2) * sin_ref[...]
        sq = rot * rot
        for h, keep_left in ((2 * j, True), (2 * j + 1, False)):
            qn = qn_ref[0, :, h * NOPE:(h + 1) * NOPE]
            sel = left if keep_left else jnp.logical_not(left)
            ss = jnp.sum(qn * qn, axis=-1, keepdims=True) + jnp.sum(jnp.where(sel, sq, 0.0), axis=-1, keepdims=True)
            inv = lax.rsqrt(ss * (1.0 / (NOPE + R)) + EPS) * scale
            o_ref[0, :, h * 2 * LANES:h * 2 * LANES + NOPE] = (qn * inv * gn_ref[...]).astype(o_ref.dtype)
            o_ref[0, :, h * 2 * LANES + NOPE:(h + 1) * 2 * LANES] = (
                jnp.where(sel, rot * inv * gr_ref[...], 0.0)).astype(o_ref.dtype)


def _mla_q(pm, cos2, sin2, q_norm_g, *, HM, NOPE, R):
    b, T, _ = pm.shape
    assert NOPE == LANES and 2 * R == LANES and HM % 2 == 0
    tb = _pick(T, (256, 128, 64))
    scale = LOG2E / math.sqrt(NOPE + R)
    gn = q_norm_g[:NOPE].reshape(1, NOPE)
    gr = jnp.tile(q_norm_g[NOPE:], 2).reshape(1, LANES)
    nope_w, rope_w = HM * NOPE, HM * R
    assert nope_w % rope_w == 0
    return pl.pallas_call(
        functools.partial(_mla_q_kernel, HM=HM, NOPE=NOPE, R=R, scale=scale),
        grid=(b, T // tb),
        in_specs=[
            pl.BlockSpec((1, tb, nope_w), lambda bi, t: (bi, t, 0)),
            pl.BlockSpec((1, tb, rope_w), lambda bi, t: (bi, t, nope_w // rope_w)),
            pl.BlockSpec((tb, LANES), lambda bi, t: (t, 0)),
            pl.BlockSpec((tb, LANES), lambda bi, t: (t, 0)),
            pl.BlockSpec((1, NOPE), lambda bi, t: (0, 0)),
            pl.BlockSpec((1, LANES), lambda bi, t: (0, 0)),
        ],
        out_specs=pl.BlockSpec((1, tb, HM * 2 * LANES), lambda bi, t: (bi, t, 0)),
        out_shape=jax.ShapeDtypeStruct((b, T, HM * 2 * LANES), BF16),
        compiler_params=_params("parallel", "parallel"),
        name="mla_q_prep",
    )(pm, pm, cos2, sin2, gn, gr)


def _mla_kv_kernel(ckv_ref, kr_ref, w_ref, gn_ref, gr_ref, k_ref, vt_ref, *, HM, NOPE, R, V):
    ckv = ckv_ref[0, 0].astype(BF16)
    kv = jnp.dot(ckv, w_ref[...], preferred_element_type=F32)
    kr = kr_ref[0, 0]
    ss_r = jnp.sum(kr * kr, axis=-1, keepdims=True)
    krg = kr * gr_ref[...]
    zero = jnp.zeros_like(krg)
    kr_even = jnp.concatenate([krg, zero], axis=1)
    kr_odd = jnp.concatenate([zero, krg], axis=1)
    for h in range(HM):
        base = h * (NOPE + V)
        kn = kv[:, base:base + NOPE]
        ss = jnp.sum(kn * kn, axis=-1, keepdims=True) + ss_r
        inv = lax.rsqrt(ss * (1.0 / (NOPE + R)) + EPS)
        k_ref[0, :, h * 2 * LANES:h * 2 * LANES + NOPE] = (kn * inv * gn_ref[...]).astype(k_ref.dtype)
        k_ref[0, :, h * 2 * LANES + NOPE:(h + 1) * 2 * LANES] = (
            (kr_even if h % 2 == 0 else kr_odd) * inv).astype(k_ref.dtype)
        vt_ref[0, 0, h * V:(h + 1) * V, :] = kv[:, base + NOPE:base + NOPE + V].T.astype(vt_ref.dtype)


def _mla_kv(ckv, krope, layer, ts, w_ukv, k_norm_g, *, HM, NOPE, R, V):
    _, b, S, KVL = ckv.shape
    assert S % ts == 0
    gn = k_norm_g[:NOPE].reshape(1, NOPE)
    gr = k_norm_g[NOPE:].reshape(1, R)
    return pl.pallas_call(
        functools.partial(_mla_kv_kernel, HM=HM, NOPE=NOPE, R=R, V=V),
        grid=(b, S // ts),
        in_specs=[
            pl.BlockSpec((1, 1, ts, KVL), lambda bi, s: (layer, bi, s, 0)),
            pl.BlockSpec((1, 1, ts, R), lambda bi, s: (layer, bi, s, 0)),
            pl.BlockSpec((KVL, HM * (NOPE + V)), lambda bi, s: (0, 0)),
            pl.BlockSpec((1, NOPE), lambda bi, s: (0, 0)),
            pl.BlockSpec((1, R), lambda bi, s: (0, 0)),
        ],
        out_specs=[
            pl.BlockSpec((1, ts, HM * 2 * LANES), lambda bi, s: (bi, s, 0)),
            pl.BlockSpec((1, 1, HM * V, ts), lambda bi, s: (bi, s, 0, 0)),
        ],
        out_shape=[jax.ShapeDtypeStruct((b, S, HM * 2 * LANES), BF16),
                   jax.ShapeDtypeStruct((b, S // ts, HM * V, ts), BF16)],
        compiler_params=_params("parallel", "parallel"),
        name="mla_kv_prep",
    )(ckv, krope, w_ukv, gn, gr)


def _attn_blocks(T, past, wide=False):
    qb = min(T, 4 * Q_BLOCK)
    kmax = (4 if wide else 2) * Q_BLOCK
    kb = min(qb, kmax)
    kbc = _pick(past, (kmax, 2 * Q_BLOCK, Q_BLOCK, CHUNK)) if past else kb
    assert T % qb == 0 and qb % kb == 0 and kb % CHUNK == 0 and past % CHUNK == 0 and past % kbc == 0
    assert T % Q_BLOCK == 0 or T <= Q_BLOCK
    return qb, kb, kbc


def _mla_tile(q, k, vt, m, l, acc, mask):
    s = lax.dot_general(k, q, (((1,), (1,)), ((), ())), preferred_element_type=F32)
    if mask is not None:
        s = jnp.where(mask, s, NEG_BIG)
    m_new = jnp.maximum(m, jnp.max(s, axis=0, keepdims=True))
    p = jnp.exp2(s - m_new)
    alpha = jnp.exp2(m - m_new)
    l = alpha * l + jnp.sum(p, axis=0, keepdims=True)
    acc = alpha * acc + jnp.dot(vt, p.astype(BF16), preferred_element_type=F32)
    return m_new, l, acc


def _mla_attn_kernel(*refs, QB, KB, KBC, n_cache, single, shift):
    if n_cache:
        q_ref, kn_ref, vtn_ref, kc_ref, vtc_ref, g_ref, o_ref = refs
    else:
        q_ref, kn_ref, vtn_ref, g_ref, o_ref = refs
    qi = pl.program_id(2)
    q = q_ref[0]
    dv = vtn_ref.shape[2]
    m = jnp.full((1, QB), NEG_BIG, F32)
    l = jnp.zeros((1, QB), F32)
    acc = jnp.zeros((dv, QB), F32)
    nd = QB // KB
    for kk in range(nd):
        r0 = kk * KB
        rows = QB - r0
        start = r0 if single else pl.multiple_of(qi * QB + r0, KB)
        k = kn_ref[0, pl.ds(start, KB), :]
        vt = vtn_ref[0, kk if single else qi * nd + kk]
        kchunk = lax.shift_right_logical(lax.broadcasted_iota(jnp.int32, (KB, rows), 0), shift)
        rchunk = lax.shift_right_logical(lax.broadcasted_iota(jnp.int32, (KB, rows), 1), shift)
        mn, ln, an = _mla_tile(q[r0:], k, vt, m[:, r0:], l[:, r0:], acc[:, r0:], kchunk <= rchunk)
        if r0:
            mn, ln, an = (jnp.concatenate([a[:, :r0], b], axis=1) for a, b in ((m, mn), (l, ln), (acc, an)))
        m, l, acc = mn, ln, an

    def body_new(j, carry):
        start = pl.multiple_of(j * KB, KB)
        return _mla_tile(q, kn_ref[0, pl.ds(start, KB), :], vtn_ref[0, j], *carry, None)

    m, l, acc = lax.fori_loop(0, qi * nd, body_new, (m, l, acc))
    if n_cache:
        def body_cache(j, carry):
            start = pl.multiple_of(j * KBC, KBC)
            return _mla_tile(q, kc_ref[0, pl.ds(start, KBC), :], vtc_ref[0, j], *carry, None)

        m, l, acc = lax.fori_loop(0, n_cache, body_cache, (m, l, acc))
    o_ref[0] = ((acc * (1.0 / l)).T * _silu(g_ref[0])).astype(o_ref.dtype)


def _mla_attn(q_full, k_new, vt_new, k_cache, vt_cache, pm, *, HM, V, g_block0):
    b, T, _ = q_full.shape
    assert CHUNK & (CHUNK - 1) == 0 and V == LANES
    past = 0 if k_cache is None else k_cache.shape[1]
    QB, KB, KBC = _attn_blocks(T, past, wide=True)
    assert vt_new.shape[-1] == KB and (not past or vt_cache.shape[-1] == KBC)
    kern = functools.partial(_mla_attn_kernel, QB=QB, KB=KB, KBC=KBC, n_cache=past // KBC, single=T == QB,
                             shift=CHUNK.bit_length() - 1)
    in_specs = [
        pl.BlockSpec((1, QB, 2 * LANES), lambda bi, h, t: (bi, t, h)),
        pl.BlockSpec((1, T, 2 * LANES), lambda bi, h, t: (bi, 0, h)),
        pl.BlockSpec((1, T // KB, V, KB), lambda bi, h, t: (bi, 0, h, 0)),
    ]
    args = [q_full, k_new, vt_new]
    if past:
        in_specs += [pl.BlockSpec((1, past, 2 * LANES), lambda bi, h, t: (bi, 0, h)),
                     pl.BlockSpec((1, past // KBC, V, KBC), lambda bi, h, t: (bi, 0, h, 0))]
        args += [k_cache, vt_cache]
    in_specs.append(pl.BlockSpec((1, QB, V), lambda bi, h, t: (bi, t, g_block0 + h)))
    args.append(pm)
    return pl.pallas_call(
        kern,
        grid=(b, HM, T // QB),
        in_specs=in_specs,
        out_specs=pl.BlockSpec((1, QB, V), lambda bi, h, t: (bi, t, h)),
        out_shape=jax.ShapeDtypeStruct((b, T, HM * V), BF16),
        compiler_params=_params("parallel", "parallel", "arbitrary"),
        name="mla_attention",
    )(*args)


def _strict_lower(n):
    row = lax.broadcasted_iota(jnp.int32, (2 * n, n), 0)
    col = lax.broadcasted_iota(jnp.int32, (2 * n, n), 1)
    return (jnp.where(row >= n, row - n, row) > col).astype(BF16)


def _sb_tile(q, k, v, run, acc, tri2, causal):
    z = lax.dot_general(q, k, (((1,), (1,)), ((), ())), preferred_element_type=F32)
    neg_abs = lax.bitcast_convert_type(lax.bitcast_convert_type(z, jnp.uint32) | jnp.uint32(0x80000000), F32)
    sp = jnp.log2(1.0 + jnp.exp2(neg_abs)) + jnp.maximum(z, 0.0)
    logsig = z - sp
    if causal:
        mask = lax.broadcasted_iota(jnp.int32, z.shape, 1) < lax.broadcasted_iota(jnp.int32, z.shape, 0)
        sp = jnp.where(mask, sp, 0.0)
    hi = sp.astype(BF16)
    lo = (sp - hi.astype(F32)).astype(BF16)
    later = jnp.dot(jnp.concatenate([hi, lo], axis=1), tri2, preferred_element_type=F32)
    att = jnp.exp2(logsig - later - run)
    if causal:
        att = jnp.where(mask, att, 0.0)
    acc = acc + jnp.dot(att.astype(BF16), v, preferred_element_type=F32)
    run = run + jnp.sum(sp, axis=-1, keepdims=True)
    return run, acc


def _sb_attn_kernel(*refs, QB, KB, KBC, n_cache, single, scale):
    if n_cache:
        q_ref, kn_ref, vn_ref, kc_ref, vc_ref, g_ref, o_ref = refs
    else:
        q_ref, kn_ref, vn_ref, g_ref, o_ref = refs
    qi = pl.program_id(2)
    q = (q_ref[0] * scale).astype(BF16)
    d = q.shape[-1]
    tri = _strict_lower(KB)
    run = jnp.zeros((QB, 1), F32)
    acc = jnp.zeros((QB, d), F32)
    nd = QB // KB
    for kk in reversed(range(nd)):
        r0 = kk * KB
        start = r0 if single else pl.multiple_of(qi * QB + r0, KB)
        k = kn_ref[0, 0, pl.ds(start, KB), :].astype(BF16)
        v = vn_ref[0, 0, pl.ds(start, KB), :].astype(BF16)
        rn, an = _sb_tile(q[r0:], k, v, run[r0:], acc[r0:], tri, True)
        if r0:
            rn, an = jnp.concatenate([run[:r0], rn], axis=0), jnp.concatenate([acc[:r0], an], axis=0)
        run, acc = rn, an

    def sweep(carry, k_ref, v_ref, n, per_trip, kb, tri_kb):
        def body(i, carry):
            for u in range(per_trip):
                start = pl.multiple_of((n - 1 - (i * per_trip + u)) * kb, kb)
                k = k_ref[0, 0, pl.ds(start, kb), :].astype(BF16)
                v = v_ref[0, 0, pl.ds(start, kb), :].astype(BF16)
                carry = _sb_tile(q, k, v, *carry, tri_kb, False)
            return carry

        return lax.fori_loop(0, n // per_trip, body, carry)

    run, acc = sweep((run, acc), kn_ref, vn_ref, qi * nd, 2 if nd % 2 == 0 else 1, KB, tri)
    if n_cache:
        tri_c = tri if KBC == KB else _strict_lower(KBC)
        run, acc = sweep((run, acc), kc_ref, vc_ref, n_cache, 2 if n_cache % 2 == 0 else 1, KBC, tri_c)
    o_ref[0] = (acc * _silu(g_ref[0])).astype(o_ref.dtype)


def _sb_attn(psb, k_new, v_new, k_cache, v_cache, layer, *, HS, DS):
    b, T, _ = psb.shape
    assert DS == LANES
    past = 0 if k_cache is None else k_cache.shape[2]
    QB, KB, KBC = _attn_blocks(T, past)
    kern = functools.partial(_sb_attn_kernel, QB=QB, KB=KB, KBC=KBC, n_cache=past // KBC, single=T == QB,
                             scale=LOG2E / math.sqrt(DS))
    in_specs = [
        pl.BlockSpec((1, QB, DS), lambda bi, h, t: (bi, t, h)),
        pl.BlockSpec((1, 1, T, DS), lambda bi, h, t: (layer, bi, 0, h)),
        pl.BlockSpec((1, 1, T, DS), lambda bi, h, t: (layer, bi, 0, h)),
    ]
    args = [psb, k_new, v_new]
    if past:
        in_specs += [pl.BlockSpec((1, 1, past, DS), lambda bi, h, t: (layer, bi, 0, h))] * 2
        args += [k_cache, v_cache]
    in_specs.append(pl.BlockSpec((1, QB, DS), lambda bi, h, t: (bi, t, HS + h)))
    args.append(psb)
    return pl.pallas_call(
        kern,
        grid=(b, HS, T // QB),
        in_specs=in_specs,
        out_specs=pl.BlockSpec((1, QB, DS), lambda bi, h, t: (bi, t, h)),
        out_shape=jax.ShapeDtypeStruct((b, T, HS * DS), BF16),
        compiler_params=_params("parallel", "parallel", "arbitrary"),
        name="sb_attention",
    )(*args)


def _merge_kernel(ya_ref, yb_ref, yc_ref, w_ref, g0_ref, g1_ref, g2_ref, bm_ref, o_ref):
    acc = None
    for i, (y_ref, g_ref) in enumerate(((ya_ref, g0_ref), (yb_ref, g1_ref), (yc_ref, g2_ref))):
        gate = _sigmoid(g_ref[...] + bm_ref[i])
        t = gate * jnp.dot(y_ref[...], w_ref[i], preferred_element_type=F32)
        acc = t if acc is None else acc + t
    o_ref[...] = acc.astype(o_ref.dtype)


def _merge(ya, yb, yc, w_branch, gate_logit, b_merge):
    m, k = ya.shape
    nb, _, d = w_branch.shape
    tm = _pick(m, (512, 256, 128, 64, 32, 16))
    tn = _pick(d, (512, 256, 128))
    nt = d // tn
    yspec = pl.BlockSpec((tm, k), lambda i, j: (i, 0))
    gspecs = [pl.BlockSpec((tm, tn), functools.partial(lambda i, j, br: (i, br * nt + j), br=br)) for br in range(nb)]
    return pl.pallas_call(
        _merge_kernel,
        grid=(m // tm, nt),
        in_specs=[yspec, yspec, yspec, pl.BlockSpec((nb, k, tn), lambda i, j: (0, 0, j))] + gspecs
        + [pl.BlockSpec((nb, 1, tn), lambda i, j: (0, 0, j))],
        out_specs=pl.BlockSpec((tm, tn), lambda i, j: (i, j)),
        out_shape=jax.ShapeDtypeStruct((m, d), BF16),
        compiler_params=_params("parallel", "arbitrary"),
        name="branch_merge",
    )(ya, yb, yc, w_branch, gate_logit, gate_logit, gate_logit, b_merge.reshape(nb, 1, d))


def _rope_tables(past, T, R):
    half = R // 2
    freq = ROPE_THETA ** (-jnp.arange(half, dtype=F32) / half)
    ang = (past + jnp.arange(T)).astype(F32)[:, None] * freq[None, :]
    cos, sin = jnp.cos(ang), jnp.sin(ang)
    reps = LANES // R
    return jnp.tile(jnp.concatenate([cos, cos], axis=1), (1, reps)), jnp.tile(jnp.concatenate([-sin, sin], axis=1), (1, reps))


def _prep_layer(dims, norm_g, w_in, conv_w, conv_b, dt_bias, a_log, d_skip, ssd_norm_g, kv_norm_g, w_ukv,
                q_norm_g, k_norm_g, w_branch, b_merge, w_out):
    D = w_in.shape[0]
    W, C, H, P = dims["W"], dims["C"], dims["H"], dims["P"]
    HM, NOPE, R, KVL, HS, DS = dims["HM"], dims["NOPE"], dims["R"], dims["KVL"], dims["HS"], dims["DS"]
    sizes = (W, C, H, HM * (NOPE + R), KVL, R, HM * dims["V"], HS * DS, HS * DS, HS * DS, HS * DS, w_branch.shape[0] * D)
    offs = [0]
    for s in sizes:
        offs.append(offs[-1] + s)
    col = lambda i: w_in[:, offs[i]:offs[i + 1]]
    pad_to = lambda w, n: jnp.pad(w, ((0, 0), (0, n - w.shape[1])))
    wq = col(3).reshape(D, HM, NOPE + R)
    w_mla = jnp.concatenate([wq[:, :, :NOPE].reshape(D, HM * NOPE), wq[:, :, NOPE:].reshape(D, HM * R), col(6)], axis=1)
    w_misc = jnp.concatenate([col(4), pad_to(col(2), LANES), pad_to(col(5), LANES)], axis=1)
    vec = lambda v: jnp.pad(v, (0, LANES - v.shape[0])).reshape(1, LANES)
    return dict(
        norm_g=norm_g,
        w_z=col(0).astype(BF16), w_xbc=col(1).astype(BF16), w_misc=w_misc.astype(BF16), w_mla=w_mla.astype(BF16),
        w_sb=jnp.concatenate([col(7), col(10)], axis=1).astype(BF16),
        w_k=col(8).astype(BF16), w_v=col(9).astype(BF16), w_gate=col(11).astype(BF16),
        conv_w=conv_w, conv_b=conv_b.reshape(1, C), dt_bias=vec(dt_bias), a_log=vec(a_log),
        dskip_x=jnp.repeat(d_skip, P).reshape(1, W), ssd_norm_g=ssd_norm_g.reshape(1, W),
        kv_norm_g=kv_norm_g, w_ukv=w_ukv.astype(BF16), q_norm_g=q_norm_g, k_norm_g=k_norm_g,
        w_branch=w_branch.astype(BF16), b_merge=b_merge, w_out=w_out.astype(BF16),
    )


def _layer(dims, p, h, layer, depth, k_stack, v_stack, sb_k_cache, sb_v_cache, ckv_cache, krope_cache, ssd_h0, conv_ctx):
    b, T, D = h.shape
    M = b * T
    H, P, G, N = dims["H"], dims["P"], dims["G"], dims["N"]
    HM, NOPE, R, V, KVL, HS, DS = dims["HM"], dims["NOPE"], dims["R"], dims["V"], dims["KVL"], dims["HS"], dims["DS"]
    hg = H // G
    past = 0 if sb_k_cache is None else sb_k_cache.shape[2]

    u = _rmsnorm(h.reshape(M, D), p["norm_g"])
    z = _matmul(u, p["w_z"], name="proj_z").reshape(b, T, -1)
    xbc = _matmul(u, p["w_xbc"], name="proj_xbc").reshape(b, T, -1)
    misc = _matmul(u, p["w_misc"], name="proj_misc").reshape(b, T, -1)
    pm = _matmul(u, p["w_mla"], name="proj_mla").reshape(b, T, -1)
    psb = _matmul(u, p["w_sb"], name="proj_sb").reshape(b, T, -1)
    k_stack = _matmul(u, p["w_k"], name="proj_sbk", stack=(k_stack, layer, depth))
    v_stack = _matmul(u, p["w_v"], name="proj_sbv", stack=(v_stack, layer, depth))
    gate_logit = _matmul(u, p["w_gate"], name="proj_gate")

    KW = p["conv_w"].shape[0]
    assert T >= KW - 1
    if ssd_h0 is None:
        h0_t = jnp.zeros((b, G, N, hg * P), F32)
        ctx = jnp.zeros((b, KW - 1, xbc.shape[-1]), F32)
    else:
        h0_t = jnp.swapaxes(ssd_h0.reshape(b, G, hg * P, N), 2, 3)
        ctx = conv_ctx
    y_a, st_t = _ssd_branch(z, xbc, misc, KVL // LANES, ctx, h0_t, p["conv_w"], p["conv_b"], p["dt_bias"],
                            p["a_log"], p["dskip_x"], p["ssd_norm_g"], H=H, P=P, G=G, N=N)
    ssd_new = jnp.swapaxes(st_t, 2, 3).reshape(b, H, P, N)
    conv_new = xbc[:, T - (KW - 1):, :]

    cos2, sin2 = _rope_tables(past, T, R)
    ckv, krope = _mla_new(misc, cos2, sin2, p["kv_norm_g"], KVL=KVL, R=R, kr_block=KVL // LANES + 1)
    q_full = _mla_q(pm, cos2, sin2, p["q_norm_g"], HM=HM, NOPE=NOPE, R=R)
    kv_args = dict(HM=HM, NOPE=NOPE, R=R, V=V)
    _, kb_new, kb_cache = _attn_blocks(T, past, wide=True)
    k_new, vt_new = _mla_kv(ckv[None], krope[None], 0, kb_new, p["w_ukv"], p["k_norm_g"], **kv_args)
    k_old, vt_old = (_mla_kv(ckv_cache, krope_cache, layer, kb_cache, p["w_ukv"], p["k_norm_g"], **kv_args)
                     if past else (None, None))
    y_b = _mla_attn(q_full, k_new, vt_new, k_old, vt_old, pm, HM=HM, V=V, g_block0=(HM * (NOPE + R)) // V)

    if past:
        k_cache = sb_k_cache.reshape(depth, b, past, HS * DS)
        v_cache = sb_v_cache.reshape(depth, b, past, HS * DS)
    else:
        k_cache = v_cache = None
    y_c = _sb_attn(psb, k_stack.reshape(depth, b, T, HS * DS), v_stack.reshape(depth, b, T, HS * DS),
                   k_cache, v_cache, layer, HS=HS, DS=DS)

    mixed = _merge(y_a.reshape(M, -1), y_b.reshape(M, -1), y_c.reshape(M, -1), p["w_branch"], gate_logit, p["b_merge"])
    h_new = _matmul(mixed, p["w_out"], residual=h.reshape(M, D), name="out_proj").reshape(b, T, D)
    return h_new, k_stack, v_stack, (ckv, krope, ssd_new, conv_new)


def kernel(x_prompt, x_sample, cache_sb_k, cache_sb_v, cache_mla_ckv, cache_mla_krope, state_ssd, state_conv,
           norm_g, w_in, conv_w, conv_b, dt_bias, a_log, d_skip, ssd_norm_g, mla_kv_norm_g, w_ukv,
           mla_q_norm_g, mla_k_norm_g, w_branch, b_merge, w_out):
    depth = w_in.shape[0]
    H, P, N = state_ssd.shape[2:]
    C = conv_w.shape[-1]
    W = H * P
    HS, DS = cache_sb_k.shape[3:]
    KVL = cache_mla_ckv.shape[-1]
    R = cache_mla_krope.shape[-1]
    NOPE = mla_q_norm_g.shape[-1] - R
    HM = (w_ukv.shape[-1] - w_branch.shape[2]) // NOPE
    V = w_ukv.shape[-1] // HM - NOPE
    dims = dict(H=H, P=P, N=N, C=C, W=W, G=(C - W) // (2 * N), HS=HS, DS=DS, KVL=KVL, R=R, NOPE=NOPE, HM=HM, V=V)
    assert KVL % LANES == 0 and HM * V == HS * DS == W

    hp, hs = x_prompt, x_sample
    kp = vp = ks = vs = None
    outs_p, outs_s = [], []
    for l in range(depth):
        p = _prep_layer(dims, norm_g[l], w_in[l], conv_w[l], conv_b[l], dt_bias[l], a_log[l], d_skip[l], ssd_norm_g[l],
                        mla_kv_norm_g[l], w_ukv[l], mla_q_norm_g[l], mla_k_norm_g[l], w_branch[l], b_merge[l], w_out[l])
        hp, kp, vp, st_p = _layer(dims, p, hp, l, depth, kp, vp, None, None, None, None, None, None)
        outs_p.append(st_p)
        hs, ks, vs, st_s = _layer(dims, p, hs, l, depth, ks, vs, cache_sb_k, cache_sb_v, cache_mla_ckv,
                                  cache_mla_krope, state_ssd[l], state_conv[l])
        outs_s.append(st_s)
    kv_shape = lambda a, h: a.reshape(depth, h.shape[0], h.shape[1], HS, DS)
    stack_p = [jnp.stack(s) for s in zip(*outs_p)]
    stack_s = [jnp.stack(s) for s in zip(*outs_s)]
    return (hp, hs, kv_shape(kp, hp), kv_shape(vp, hp), *stack_p, kv_shape(ks, hs), kv_shape(vs, hs), *stack_s)
```

```python
import functools
import math

import jax
import jax.numpy as jnp
from jax import lax
from jax.experimental import pallas as pl
from jax.experimental.pallas import tpu as pltpu

CHUNK = 64
Q_BLOCK = 128
EPS = 1e-6
ROPE_THETA = 10000.0
LANES = 128
NEG_BIG = -1e30
LOG2E = 1.4426950408889634
UNDERFLOW_LOG2 = 160.0
VMEM_LIMIT_BYTES = 48 * 1024 * 1024

F32 = jnp.float32
BF16 = jnp.bfloat16


def _params(*sem, flags=None):
    return pltpu.CompilerParams(dimension_semantics=sem, vmem_limit_bytes=VMEM_LIMIT_BYTES, flags=flags)


def _pick(n, prefs):
    for p in prefs:
        if n % p == 0:
            return p
    return n


def _sigmoid(x):
    return 1.0 / (1.0 + jnp.exp(-x))


def _silu(x):
    return x * _sigmoid(x)


def _softplus(x):
    return jnp.maximum(x, 0.0) + jnp.log(1.0 + jnp.exp(-jnp.abs(x)))


def _rmsnorm_kernel(x_ref, g_ref, o_ref):
    x = x_ref[...]
    ms = jnp.mean(x * x, axis=-1, keepdims=True)
    o_ref[...] = (x * lax.rsqrt(ms + EPS) * g_ref[...]).astype(o_ref.dtype)


def _rmsnorm(x2d, g):
    m, d = x2d.shape
    tm = _pick(m, (512, 256, 128, 64, 32, 16))
    return pl.pallas_call(
        _rmsnorm_kernel,
        grid=(m // tm,),
        in_specs=[pl.BlockSpec((tm, d), lambda i: (i, 0)), pl.BlockSpec((1, d), lambda i: (0, 0))],
        out_specs=pl.BlockSpec((tm, d), lambda i: (i, 0)),
        out_shape=jax.ShapeDtypeStruct((m, d), BF16),
        compiler_params=_params("parallel"),
        name="rmsnorm",
    )(x2d, g.reshape(1, d))


def _mm_kernel(x_ref, w_ref, o_ref):
    o_ref[...] = jnp.dot(x_ref[...], w_ref[...], preferred_element_type=F32).astype(o_ref.dtype)


def _mm_res_kernel(x_ref, w_ref, r_ref, o_ref):
    o_ref[...] = r_ref[...] + jnp.dot(x_ref[...], w_ref[...], preferred_element_type=F32)


def _mm_into_kernel(x_ref, w_ref, prev_ref, o_ref):
    del prev_ref
    _mm_kernel(x_ref, w_ref, o_ref)


def _matmul(x, w, residual=None, name="matmul", stack=None):
    m, k = x.shape
    n = w.shape[1]
    tm = _pick(m, (1024, 512, 256, 128, 64, 32, 16))
    tn = _pick(n, (1024, 512, 256, 128))
    in_specs = [pl.BlockSpec((tm, k), lambda i, j: (i, 0)), pl.BlockSpec((k, tn), lambda i, j: (0, j))]
    args = [x, w]
    kern = _mm_kernel
    out_spec = pl.BlockSpec((tm, tn), lambda i, j: (i, j))
    out_shape = jax.ShapeDtypeStruct((m, n), F32)
    aliases = {}
    if residual is not None:
        in_specs.append(pl.BlockSpec((tm, tn), lambda i, j: (i, j)))
        args.append(residual)
        kern = _mm_res_kernel
    if stack is not None:
        prev, layer, depth = stack
        out_spec = pl.BlockSpec((None, tm, tn), lambda i, j: (layer, i, j))
        out_shape = jax.ShapeDtypeStruct((depth, m, n), F32)
        if prev is not None:
            in_specs.append(pl.BlockSpec(memory_space=pl.ANY))
            args.append(prev)
            kern = _mm_into_kernel
            aliases = {2: 0}
    return pl.pallas_call(
        kern,
        grid=(m // tm, n // tn),
        in_specs=in_specs,
        out_specs=out_spec,
        out_shape=out_shape,
        input_output_aliases=aliases,
        compiler_params=_params("parallel", "arbitrary"),
        name=name,
    )(*args)


def _ssd_kernel(z_ref, xbc_ref, dt_ref, ctx_ref, h0_ref, cw_ref, cb_ref, dtb_ref, alog_ref, dskip_ref, ng_ref,
                y_ref, st_ref, pad_scr, y_scr, *, L, H, P, G, N, KW):
    c = pl.program_id(1)
    W = H * P
    hg = H // G
    gw = hg * P
    top = 8 - (KW - 1)

    @pl.when(c == 0)
    def _():
        pad_scr[top:8, :] = ctx_ref[0]
        st_ref[0] = h0_ref[0]

    pad_scr[8:8 + L, :] = xbc_ref[0]
    conv = cb_ref[...] + cw_ref[0:1, :] * pad_scr[top:top + L, :]
    for j in range(1, KW):
        conv = conv + cw_ref[j:j + 1, :] * pad_scr[top + j:top + j + L, :]
    pad_scr[top:8, :] = pad_scr[8 + L - (KW - 1):8 + L, :]
    conv = _silu(conv)
    xs = conv[:, :W]
    bm = conv[:, W:W + G * N]
    cm = conv[:, W + G * N:]

    dt = _softplus(dt_ref[0] + dtb_ref[...])
    a = -jnp.exp(alog_ref[...])
    da = dt * a
    row = lax.broadcasted_iota(jnp.int32, (L, L), 0)
    col = lax.broadcasted_iota(jnp.int32, (L, L), 1)
    tril = col <= row
    cum = jnp.dot(tril.astype(F32), da, precision=lax.Precision.HIGHEST, preferred_element_type=F32)
    cum_t = cum.T
    dt_t = dt.T
    cum_last = cum[L - 1:L, :]
    wend = jnp.exp(cum_last - cum) * dt
    ecum = jnp.exp(cum)
    cdec = jnp.exp(cum_last)
    left = lax.broadcasted_iota(jnp.int32, (L, LANES), 1) < P
    left1 = lax.broadcasted_iota(jnp.int32, (1, LANES), 1) < P

    for g in range(G):
        bg = bm[:, g * N:(g + 1) * N].astype(BF16)
        cg = cm[:, g * N:(g + 1) * N].astype(BF16)
        cbm = lax.dot_general(cg, bg, (((1,), (1,)), ((), ())), preferred_element_type=F32)
        st = st_ref[0, g]
        yoff = jnp.dot(cg, st.astype(BF16), preferred_element_type=F32)
        xw_parts, cd_parts = [], []
        for j in range(hg // 2):
            h1 = g * hg + 2 * j
            h2 = h1 + 1
            xp = xs[:, h1 * P:h1 * P + LANES]
            yp = None
            for h, keep_left in ((h1, True), (h2, False)):
                diff = cum[:, h:h + 1] - cum_t[h:h + 1, :]
                w = cbm * jnp.exp(jnp.where(tril, diff, NEG_BIG)) * dt_t[h:h + 1, :]
                xh = jnp.where(left, xp, 0.0) if keep_left else jnp.where(left, 0.0, xp)
                t = jnp.dot(w.astype(BF16), xh.astype(BF16), preferred_element_type=F32)
                yp = t if yp is None else yp + t
            e = jnp.where(left, ecum[:, h1:h1 + 1], ecum[:, h2:h2 + 1])
            y_scr[:, h1 * P:h1 * P + LANES] = yp + e * yoff[:, j * LANES:(j + 1) * LANES]
            we = jnp.where(left, wend[:, h1:h1 + 1], wend[:, h2:h2 + 1])
            xw_parts.append(xp * we)
            cd_parts.append(jnp.where(left1, cdec[:, h1:h1 + 1], cdec[:, h2:h2 + 1]))
        xw = jnp.concatenate(xw_parts, axis=1).astype(BF16)
        cd = jnp.concatenate(cd_parts, axis=1)
        upd = lax.dot_general(bg, xw, (((0,), (0,)), ((), ())), preferred_element_type=F32)
        st_ref[0, g] = cd * st + upd

    y = (y_scr[...] + dskip_ref[...] * xs) * _silu(z_ref[0])
    for g in range(G):
        yg = y[:, g * gw:(g + 1) * gw]
        ms = jnp.mean(yg * yg, axis=-1, keepdims=True)
        y_ref[0, :, g * gw:(g + 1) * gw] = (
            yg * lax.rsqrt(ms + EPS) * ng_ref[:, g * gw:(g + 1) * gw]).astype(y_ref.dtype)


def _ssd_branch(z, xbc, misc, dt_block, ctx, h0_t, cw, cb, dtb, alog, dskip_x, ng, *, H, P, G, N):
    b, T, W = z.shape
    C = xbc.shape[-1]
    KW = cw.shape[0]
    L = min(CHUNK, T)
    hg = H // G
    gw = hg * P
    assert T % L == 0 and L >= KW - 1 and hg % 2 == 0 and 2 * P == LANES and H <= LANES
    kern = functools.partial(_ssd_kernel, L=L, H=H, P=P, G=G, N=N, KW=KW)
    const2 = lambda bi, c: (0, 0)
    return pl.pallas_call(
        kern,
        grid=(b, T // L),
        in_specs=[
            pl.BlockSpec((1, L, W), lambda bi, c: (bi, c, 0)),
            pl.BlockSpec((1, L, C), lambda bi, c: (bi, c, 0)),
            pl.BlockSpec((1, L, LANES), lambda bi, c: (bi, c, dt_block)),
            pl.BlockSpec((1, KW - 1, C), lambda bi, c: (bi, 0, 0)),
            pl.BlockSpec((1, G, N, gw), lambda bi, c: (bi, 0, 0, 0)),
            pl.BlockSpec((KW, C), const2),
            pl.BlockSpec((1, C), const2),
            pl.BlockSpec((1, LANES), const2),
            pl.BlockSpec((1, LANES), const2),
            pl.BlockSpec((1, W), const2),
            pl.BlockSpec((1, W), const2),
        ],
        out_specs=[
            pl.BlockSpec((1, L, W), lambda bi, c: (bi, c, 0)),
            pl.BlockSpec((1, G, N, gw), lambda bi, c: (bi, 0, 0, 0)),
        ],
        out_shape=[jax.ShapeDtypeStruct((b, T, W), BF16), jax.ShapeDtypeStruct((b, G, N, gw), F32)],
        scratch_shapes=[pltpu.VMEM((8 + L, C), F32), pltpu.VMEM((L, W), F32)],
        compiler_params=_params("parallel", "arbitrary"),
        name="ssd_scan",
    )(z, xbc, misc, ctx, h0_t, cw, cb, dtb, alog, dskip_x, ng)


def _swap_halves(x, half):
    lane = lax.broadcasted_iota(jnp.int32, x.shape, 1)
    first = (lane % (2 * half)) < half
    return jnp.where(first, pltpu.roll(x, LANES - half, 1), pltpu.roll(x, half, 1))


def _mla_new_kernel(ckv_ref, kr_ref, cos_ref, sin_ref, g_ref, ckv_out, kr_out, *, R):
    x = ckv_ref[0]
    ms = jnp.mean(x * x, axis=-1, keepdims=True)
    ckv_out[0] = x * lax.rsqrt(ms + EPS) * g_ref[...]
    kr = kr_ref[0]
    rot = kr * cos_ref[...] + _swap_halves(kr, R // 2) * sin_ref[...]
    kr_out[0] = rot[:, :R]


def _mla_new(misc, cos2, sin2, kv_norm_g, *, KVL, R, kr_block):
    b, T, _ = misc.shape
    tb = _pick(T, (256, 128, 64))
    return pl.pallas_call(
        functools.partial(_mla_new_kernel, R=R),
        grid=(b, T // tb),
        in_specs=[
            pl.BlockSpec((1, tb, KVL), lambda bi, t: (bi, t, 0)),
            pl.BlockSpec((1, tb, LANES), lambda bi, t: (bi, t, kr_block)),
            pl.BlockSpec((tb, LANES), lambda bi, t: (t, 0)),
            pl.BlockSpec((tb, LANES), lambda bi, t: (t, 0)),
            pl.BlockSpec((1, KVL), lambda bi, t: (0, 0)),
        ],
        out_specs=[
            pl.BlockSpec((1, tb, KVL), lambda bi, t: (bi, t, 0)),
            pl.BlockSpec((1, tb, R), lambda bi, t: (bi, t, 0)),
        ],
        out_shape=[jax.ShapeDtypeStruct((b, T, KVL), F32), jax.ShapeDtypeStruct((b, T, R), F32)],
        compiler_params=_params("parallel", "parallel"),
        name="mla_new_kv",
    )(misc, misc, cos2, sin2, kv_norm_g.reshape(1, KVL))


def _mla_q_kernel(qn_ref, qr_ref, cos_ref, sin_ref, gn_ref, gr_ref, o_ref, *, HM, NOPE, R, scale):
    lane = lax.broadcasted_iota(jnp.int32, (qn_ref.shape[1], LANES), 1)
    left = lane < R
    for j in range(HM // 2):
        qr = qr_ref[0, :, j * LANES:(j + 1) * LANES]
        rot = qr * cos_ref[...] + _swap_halves(qr, R // 2) * sin_ref[...]
        sq = rot * rot
        for h, keep_left in ((2 * j, True), (2 * j + 1, False)):
            qn = qn_ref[0, :, h * NOPE:(h + 1) * NOPE]
            sel = left if keep_left else jnp.logical_not(left)
            ss = jnp.sum(qn * qn, axis=-1, keepdims=True) + jnp.sum(jnp.where(sel, sq, 0.0), axis=-1, keepdims=True)
            inv = lax.rsqrt(ss * (1.0 / (NOPE + R)) + EPS) * scale
            o_ref[0, :, h * 2 * LANES:h * 2 * LANES + NOPE] = (qn * inv * gn_ref[...]).astype(o_ref.dtype)
            o_ref[0, :, h * 2 * LANES + NOPE:(h + 1) * 2 * LANES] = (
                jnp.where(sel, rot * inv * gr_ref[...], 0.0)).astype(o_ref.dtype)


def _mla_q(pm, cos2, sin2, q_norm_g, *, HM, NOPE, R):
    b, T, _ = pm.shape
    assert NOPE == LANES and 2 * R == LANES and HM % 2 == 0
    tb = _pick(T, (256, 128, 64))
    scale = LOG2E / math.sqrt(NOPE + R)
    gn = q_norm_g[:NOPE].reshape(1, NOPE)
    gr = jnp.tile(q_norm_g[NOPE:], 2).reshape(1, LANES)
    nope_w, rope_w = HM * NOPE, HM * R
    assert nope_w % rope_w == 0
    return pl.pallas_call(
        functools.partial(_mla_q_kernel, HM=HM, NOPE=NOPE, R=R, scale=scale),
        grid=(b, T // tb),
        in_specs=[
            pl.BlockSpec((1, tb, nope_w), lambda bi, t: (bi, t, 0)),
            pl.BlockSpec((1, tb, rope_w), lambda bi, t: (bi, t, nope_w // rope_w)),
            pl.BlockSpec((tb, LANES), lambda bi, t: (t, 0)),
            pl.BlockSpec((tb, LANES), lambda bi, t: (t, 0)),
            pl.BlockSpec((1, NOPE), lambda bi, t: (0, 0)),
            pl.BlockSpec((1, LANES), lambda bi, t: (0, 0)),
        ],
        out_specs=pl.BlockSpec((1, tb, HM * 2 * LANES), lambda bi, t: (bi, t, 0)),
        out_shape=jax.ShapeDtypeStruct((b, T, HM * 2 * LANES), BF16),
        compiler_params=_params("parallel", "parallel"),
        name="mla_q_prep",
    )(pm, pm, cos2, sin2, gn, gr)


def _mla_kv_kernel(ckv_ref, kr_ref, w_ref, gn_ref, gr_ref, k_ref, vt_ref, *, HM, NOPE, R, V):
    ckv = ckv_ref[0, 0].astype(BF16)
    kv = jnp.dot(ckv, w_ref[...], preferred_element_type=F32)
    kr = kr_ref[0, 0]
    ss_r = jnp.sum(kr * kr, axis=-1, keepdims=True)
    krg = kr * gr_ref[...]
    zero = jnp.zeros_like(krg)
    kr_even = jnp.concatenate([krg, zero], axis=1)
    kr_odd = jnp.concatenate([zero, krg], axis=1)
    for h in range(HM):
        base = h * (NOPE + V)
        kn = kv[:, base:base + NOPE]
        ss = jnp.sum(kn * kn, axis=-1, keepdims=True) + ss_r
        inv = lax.rsqrt(ss * (1.0 / (NOPE + R)) + EPS)
        k_ref[0, :, h * 2 * LANES:h * 2 * LANES + NOPE] = (kn * inv * gn_ref[...]).astype(k_ref.dtype)
        k_ref[0, :, h * 2 * LANES + NOPE:(h + 1) * 2 * LANES] = (
            (kr_even if h % 2 == 0 else kr_odd) * inv).astype(k_ref.dtype)
        vt_ref[0, 0, h * V:(h + 1) * V, :] = kv[:, base + NOPE:base + NOPE + V].T.astype(vt_ref.dtype)


def _mla_kv(ckv, krope, layer, ts, w_ukv, k_norm_g, *, HM, NOPE, R, V):
    _, b, S, KVL = ckv.shape
    assert S % ts == 0
    gn = k_norm_g[:NOPE].reshape(1, NOPE)
    gr = k_norm_g[NOPE:].reshape(1, R)
    return pl.pallas_call(
        functools.partial(_mla_kv_kernel, HM=HM, NOPE=NOPE, R=R, V=V),
        grid=(b, S // ts),
        in_specs=[
            pl.BlockSpec((1, 1, ts, KVL), lambda bi, s: (layer, bi, s, 0)),
            pl.BlockSpec((1, 1, ts, R), lambda bi, s: (layer, bi, s, 0)),
            pl.BlockSpec((KVL, HM * (NOPE + V)), lambda bi, s: (0, 0)),
            pl.BlockSpec((1, NOPE), lambda bi, s: (0, 0)),
            pl.BlockSpec((1, R), lambda bi, s: (0, 0)),
        ],
        out_specs=[
            pl.BlockSpec((1, ts, HM * 2 * LANES), lambda bi, s: (bi, s, 0)),
            pl.BlockSpec((1, 1, HM * V, ts), lambda bi, s: (bi, s, 0, 0)),
        ],
        out_shape=[jax.ShapeDtypeStruct((b, S, HM * 2 * LANES), BF16),
                   jax.ShapeDtypeStruct((b, S // ts, HM * V, ts), BF16)],
        compiler_params=_params("parallel", "parallel"),
        name="mla_kv_prep",
    )(ckv, krope, w_ukv, gn, gr)


def _attn_blocks(T, past):
    qb = min(T, 4 * Q_BLOCK)
    kb = min(qb, 2 * Q_BLOCK)
    kbc = _pick(past, (2 * Q_BLOCK, Q_BLOCK, CHUNK)) if past else kb
    assert T % qb == 0 and qb % kb == 0 and kb % CHUNK == 0 and past % CHUNK == 0 and past % kbc == 0
    assert T % Q_BLOCK == 0 or T <= Q_BLOCK
    return qb, kb, kbc


def _mla_tile(q, k, vt, m, l, acc, mask):
    s = lax.dot_general(k, q, (((1,), (1,)), ((), ())), preferred_element_type=F32)
    if mask is not None:
        s = jnp.where(mask, s, NEG_BIG)
    m_new = jnp.maximum(m, jnp.max(s, axis=0, keepdims=True))
    p = jnp.exp2(s - m_new)
    alpha = jnp.exp2(m - m_new)
    l = alpha * l + jnp.sum(p, axis=0, keepdims=True)
    acc = alpha * acc + jnp.dot(vt, p.astype(BF16), preferred_element_type=F32)
    return m_new, l, acc


def _mla_attn_kernel(*refs, QB, KB, KBC, n_cache, single, shift):
    if n_cache:
        q_ref, kn_ref, vtn_ref, kc_ref, vtc_ref, g_ref, o_ref = refs
    else:
        q_ref, kn_ref, vtn_ref, g_ref, o_ref = refs
    qi = pl.program_id(2)
    q = q_ref[0]
    dv = vtn_ref.shape[2]
    m = jnp.full((1, QB), NEG_BIG, F32)
    l = jnp.zeros((1, QB), F32)
    acc = jnp.zeros((dv, QB), F32)
    nd = QB // KB
    for kk in range(nd):
        r0 = kk * KB
        rows = QB - r0
        start = r0 if single else pl.multiple_of(qi * QB + r0, KB)
        k = kn_ref[0, pl.ds(start, KB), :]
        vt = vtn_ref[0, kk if single else qi * nd + kk]
        kchunk = lax.shift_right_logical(lax.broadcasted_iota(jnp.int32, (KB, rows), 0), shift)
        rchunk = lax.shift_right_logical(lax.broadcasted_iota(jnp.int32, (KB, rows), 1), shift)
        mn, ln, an = _mla_tile(q[r0:], k, vt, m[:, r0:], l[:, r0:], acc[:, r0:], kchunk <= rchunk)
        if r0:
            mn, ln, an = (jnp.concatenate([a[:, :r0], b], axis=1) for a, b in ((m, mn), (l, ln), (acc, an)))
        m, l, acc = mn, ln, an

    def sweep(carry, k_ref, vt_ref, n, per_tile, kb):
        def body(j, carry):
            start = pl.multiple_of(j * (per_tile * kb), per_tile * kb)
            k = k_ref[0, pl.ds(start, per_tile * kb), :]
            vt = jnp.concatenate([vt_ref[0, j * per_tile + u] for u in range(per_tile)], axis=1)
            return _mla_tile(q, k, vt, *carry, None)

        return lax.fori_loop(0, n // per_tile, body, carry)

    m, l, acc = sweep((m, l, acc), kn_ref, vtn_ref, qi * nd, 2 if nd % 2 == 0 else 1, KB)
    if n_cache:
        m, l, acc = sweep((m, l, acc), kc_ref, vtc_ref, n_cache, 2 if n_cache % 2 == 0 else 1, KBC)
    o_ref[0] = ((acc * (1.0 / l)).T * _silu(g_ref[0])).astype(o_ref.dtype)


def _mla_attn(q_full, k_new, vt_new, k_cache, vt_cache, pm, *, HM, V, g_block0):
    b, T, _ = q_full.shape
    assert CHUNK & (CHUNK - 1) == 0 and V == LANES
    past = 0 if k_cache is None else k_cache.shape[1]
    QB, KB, KBC = _attn_blocks(T, past)
    assert vt_new.shape[-1] == KB and (not past or vt_cache.shape[-1] == KBC)
    kern = functools.partial(_mla_attn_kernel, QB=QB, KB=KB, KBC=KBC, n_cache=past // KBC, single=T == QB,
                             shift=CHUNK.bit_length() - 1)
    in_specs = [
        pl.BlockSpec((1, QB, 2 * LANES), lambda bi, h, t: (bi, t, h)),
        pl.BlockSpec((1, T, 2 * LANES), lambda bi, h, t: (bi, 0, h)),
        pl.BlockSpec((1, T // KB, V, KB), lambda bi, h, t: (bi, 0, h, 0)),
    ]
    args = [q_full, k_new, vt_new]
    if past:
        in_specs += [pl.BlockSpec((1, past, 2 * LANES), lambda bi, h, t: (bi, 0, h)),
                     pl.BlockSpec((1, past // KBC, V, KBC), lambda bi, h, t: (bi, 0, h, 0))]
        args += [k_cache, vt_cache]
    in_specs.append(pl.BlockSpec((1, QB, V), lambda bi, h, t: (bi, t, g_block0 + h)))
    args.append(pm)
    return pl.pallas_call(
        kern,
        grid=(b, HM, T // QB),
        in_specs=in_specs,
        out_specs=pl.BlockSpec((1, QB, V), lambda bi, h, t: (bi, t, h)),
        out_shape=jax.ShapeDtypeStruct((b, T, HM * V), BF16),
        compiler_params=_params("parallel", "parallel", "arbitrary"),
        name="mla_attention",
    )(*args)


def _strict_lower(n):
    row = lax.broadcasted_iota(jnp.int32, (2 * n, n), 0)
    col = lax.broadcasted_iota(jnp.int32, (2 * n, n), 1)
    return (jnp.where(row >= n, row - n, row) > col).astype(BF16)


def _sb_tile(q, k, v, run, acc, tri2, causal):
    z = lax.dot_general(q, k, (((1,), (1,)), ((), ())), preferred_element_type=F32)
    neg_abs = lax.bitcast_convert_type(lax.bitcast_convert_type(z, jnp.uint32) | jnp.uint32(0x80000000), F32)
    sp = jnp.log2(1.0 + jnp.exp2(neg_abs)) + jnp.maximum(z, 0.0)
    logsig = z - sp
    if causal:
        mask = lax.broadcasted_iota(jnp.int32, z.shape, 1) < lax.broadcasted_iota(jnp.int32, z.shape, 0)
        sp = jnp.where(mask, sp, 0.0)
    hi = sp.astype(BF16)
    lo = (sp - hi.astype(F32)).astype(BF16)
    later = jnp.dot(jnp.concatenate([hi, lo], axis=1), tri2, preferred_element_type=F32)
    att = jnp.exp2(logsig - later - run)
    if causal:
        att = jnp.where(mask, att, 0.0)
    acc = acc + jnp.dot(att.astype(BF16), v, preferred_element_type=F32)
    run = run + jnp.sum(sp, axis=-1, keepdims=True)
    return run, acc


def _sb_attn_kernel(*refs, QB, KB, KBC, n_cache, single, scale):
    if n_cache:
        q_ref, kn_ref, vn_ref, kc_ref, vc_ref, g_ref, o_ref = refs
    else:
        q_ref, kn_ref, vn_ref, g_ref, o_ref = refs
    qi = pl.program_id(2)
    q = (q_ref[0] * scale).astype(BF16)
    d = q.shape[-1]
    tri = _strict_lower(KB)
    run = jnp.zeros((QB, 1), F32)
    acc = jnp.zeros((QB, d), F32)
    nd = QB // KB
    for kk in reversed(range(nd)):
        r0 = kk * KB
        start = r0 if single else pl.multiple_of(qi * QB + r0, KB)
        k = kn_ref[0, 0, pl.ds(start, KB), :].astype(BF16)
        v = vn_ref[0, 0, pl.ds(start, KB), :].astype(BF16)
        rn, an = _sb_tile(q[r0:], k, v, run[r0:], acc[r0:], tri, True)
        if r0:
            rn, an = jnp.concatenate([run[:r0], rn], axis=0), jnp.concatenate([acc[:r0], an], axis=0)
        run, acc = rn, an

    def sweep(carry, k_ref, v_ref, n, kb, tri_kb):
        def cond(state):
            i, run, _ = state
            return jnp.logical_and(i < n, jnp.min(run) < UNDERFLOW_LOG2)

        def body(state):
            i, run, acc = state
            start = pl.multiple_of((n - 1 - i) * kb, kb)
            k = k_ref[0, 0, pl.ds(start, kb), :].astype(BF16)
            v = v_ref[0, 0, pl.ds(start, kb), :].astype(BF16)
            return (i + 1,) + _sb_tile(q, k, v, run, acc, tri_kb, False)

        return lax.while_loop(cond, body, (jnp.int32(0),) + carry)[1:]

    run, acc = sweep((run, acc), kn_ref, vn_ref, qi * nd, KB, tri)
    if n_cache:
        run, acc = sweep((run, acc), kc_ref, vc_ref, n_cache, KBC, tri if KBC == KB else _strict_lower(KBC))
    o_ref[0] = (acc * _silu(g_ref[0])).astype(o_ref.dtype)


def _sb_attn(psb, k_new, v_new, k_cache, v_cache, layer, *, HS, DS):
    b, T, _ = psb.shape
    assert DS == LANES
    past = 0 if k_cache is None else k_cache.shape[2]
    QB, KB, KBC = _attn_blocks(T, past)
    kern = functools.partial(_sb_attn_kernel, QB=QB, KB=KB, KBC=KBC, n_cache=past // KBC, single=T == QB,
                             scale=LOG2E / math.sqrt(DS))
    in_specs = [
        pl.BlockSpec((1, QB, DS), lambda bi, h, t: (bi, t, h)),
        pl.BlockSpec((1, 1, T, DS), lambda bi, h, t: (layer, bi, 0, h)),
        pl.BlockSpec((1, 1, T, DS), lambda bi, h, t: (layer, bi, 0, h)),
    ]
    args = [psb, k_new, v_new]
    if past:
        in_specs += [pl.BlockSpec((1, 1, past, DS), lambda bi, h, t: (layer, bi, 0, h))] * 2
        args += [k_cache, v_cache]
    in_specs.append(pl.BlockSpec((1, QB, DS), lambda bi, h, t: (bi, t, HS + h)))
    args.append(psb)
    return pl.pallas_call(
        kern,
        grid=(b, HS, T // QB),
        in_specs=in_specs,
        out_specs=pl.BlockSpec((1, QB, DS), lambda bi, h, t: (bi, t, h)),
        out_shape=jax.ShapeDtypeStruct((b, T, HS * DS), BF16),
        compiler_params=_params("parallel", "parallel", "arbitrary"),
        name="sb_attention",
    )(*args)


def _merge_kernel(ya_ref, yb_ref, yc_ref, w_ref, g0_ref, g1_ref, g2_ref, bm_ref, o_ref):
    acc = None
    for i, (y_ref, g_ref) in enumerate(((ya_ref, g0_ref), (yb_ref, g1_ref), (yc_ref, g2_ref))):
        gate = _sigmoid(g_ref[...] + bm_ref[i])
        t = gate * jnp.dot(y_ref[...], w_ref[i], preferred_element_type=F32)
        acc = t if acc is None else acc + t
    o_ref[...] = acc.astype(o_ref.dtype)


def _merge(ya, yb, yc, w_branch, gate_logit, b_merge):
    m, k = ya.shape
    nb, _, d = w_branch.shape
    tm = _pick(m, (512, 256, 128, 64, 32, 16))
    tn = _pick(d, (512, 256, 128))
    nt = d // tn
    yspec = pl.BlockSpec((tm, k), lambda i, j: (i, 0))
    gspecs = [pl.BlockSpec((tm, tn), functools.partial(lambda i, j, br: (i, br * nt + j), br=br)) for br in range(nb)]
    return pl.pallas_call(
        _merge_kernel,
        grid=(m // tm, nt),
        in_specs=[yspec, yspec, yspec, pl.BlockSpec((nb, k, tn), lambda i, j: (0, 0, j))] + gspecs
        + [pl.BlockSpec((nb, 1, tn), lambda i, j: (0, 0, j))],
        out_specs=pl.BlockSpec((tm, tn), lambda i, j: (i, j)),
        out_shape=jax.ShapeDtypeStruct((m, d), BF16),
        compiler_params=_params("parallel", "arbitrary"),
        name="branch_merge",
    )(ya, yb, yc, w_branch, gate_logit, gate_logit, gate_logit, b_merge.reshape(nb, 1, d))


def _rope_tables(past, T, R):
    half = R // 2
    freq = ROPE_THETA ** (-jnp.arange(half, dtype=F32) / half)
    ang = (past + jnp.arange(T)).astype(F32)[:, None] * freq[None, :]
    cos, sin = jnp.cos(ang), jnp.sin(ang)
    reps = LANES // R
    return jnp.tile(jnp.concatenate([cos, cos], axis=1), (1, reps)), jnp.tile(jnp.concatenate([-sin, sin], axis=1), (1, reps))


def _prep_layer(dims, norm_g, w_in, conv_w, conv_b, dt_bias, a_log, d_skip, ssd_norm_g, kv_norm_g, w_ukv,
                q_norm_g, k_norm_g, w_branch, b_merge, w_out):
    D = w_in.shape[0]
    W, C, H, P = dims["W"], dims["C"], dims["H"], dims["P"]
    HM, NOPE, R, KVL, HS, DS = dims["HM"], dims["NOPE"], dims["R"], dims["KVL"], dims["HS"], dims["DS"]
    sizes = (W, C, H, HM * (NOPE + R), KVL, R, HM * dims["V"], HS * DS, HS * DS, HS * DS, HS * DS, w_branch.shape[0] * D)
    offs = [0]
    for s in sizes:
        offs.append(offs[-1] + s)
    col = lambda i: w_in[:, offs[i]:offs[i + 1]]
    pad_to = lambda w, n: jnp.pad(w, ((0, 0), (0, n - w.shape[1])))
    wq = col(3).reshape(D, HM, NOPE + R)
    w_mla = jnp.concatenate([wq[:, :, :NOPE].reshape(D, HM * NOPE), wq[:, :, NOPE:].reshape(D, HM * R), col(6)], axis=1)
    w_misc = jnp.concatenate([col(4), pad_to(col(2), LANES), pad_to(col(5), LANES)], axis=1)
    vec = lambda v: jnp.pad(v, (0, LANES - v.shape[0])).reshape(1, LANES)
    return dict(
        norm_g=norm_g,
        w_z=col(0).astype(BF16), w_xbc=col(1).astype(BF16), w_misc=w_misc.astype(BF16), w_mla=w_mla.astype(BF16),
        w_sb=jnp.concatenate([col(7), col(10)], axis=1).astype(BF16),
        w_k=col(8).astype(BF16), w_v=col(9).astype(BF16), w_gate=col(11).astype(BF16),
        conv_w=conv_w, conv_b=conv_b.reshape(1, C), dt_bias=vec(dt_bias), a_log=vec(a_log),
        dskip_x=jnp.repeat(d_skip, P).reshape(1, W), ssd_norm_g=ssd_norm_g.reshape(1, W),
        kv_norm_g=kv_norm_g, w_ukv=w_ukv.astype(BF16), q_norm_g=q_norm_g, k_norm_g=k_norm_g,
        w_branch=w_branch.astype(BF16), b_merge=b_merge, w_out=w_out.astype(BF16),
    )


def _layer(dims, p, h, layer, depth, k_stack, v_stack, sb_k_cache, sb_v_cache, ckv_cache, krope_cache, ssd_h0, conv_ctx):
    b, T, D = h.shape
    M = b * T
    H, P, G, N = dims["H"], dims["P"], dims["G"], dims["N"]
    HM, NOPE, R, V, KVL, HS, DS = dims["HM"], dims["NOPE"], dims["R"], dims["V"], dims["KVL"], dims["HS"], dims["DS"]
    hg = H // G
    past = 0 if sb_k_cache is None else sb_k_cache.shape[2]

    u = _rmsnorm(h.reshape(M, D), p["norm_g"])
    z = _matmul(u, p["w_z"], name="proj_z").reshape(b, T, -1)
    xbc = _matmul(u, p["w_xbc"], name="proj_xbc").reshape(b, T, -1)
    misc = _matmul(u, p["w_misc"], name="proj_misc").reshape(b, T, -1)
    pm = _matmul(u, p["w_mla"], name="proj_mla").reshape(b, T, -1)
    psb = _matmul(u, p["w_sb"], name="proj_sb").reshape(b, T, -1)
    k_stack = _matmul(u, p["w_k"], name="proj_sbk", stack=(k_stack, layer, depth))
    v_stack = _matmul(u, p["w_v"], name="proj_sbv", stack=(v_stack, layer, depth))
    gate_logit = _matmul(u, p["w_gate"], name="proj_gate")

    KW = p["conv_w"].shape[0]
    assert T >= KW - 1
    if ssd_h0 is None:
        h0_t = jnp.zeros((b, G, N, hg * P), F32)
        ctx = jnp.zeros((b, KW - 1, xbc.shape[-1]), F32)
    else:
        h0_t = jnp.swapaxes(ssd_h0.reshape(b, G, hg * P, N), 2, 3)
        ctx = conv_ctx
    y_a, st_t = _ssd_branch(z, xbc, misc, KVL // LANES, ctx, h0_t, p["conv_w"], p["conv_b"], p["dt_bias"],
                            p["a_log"], p["dskip_x"], p["ssd_norm_g"], H=H, P=P, G=G, N=N)
    ssd_new = jnp.swapaxes(st_t, 2, 3).reshape(b, H, P, N)
    conv_new = xbc[:, T - (KW - 1):, :]

    cos2, sin2 = _rope_tables(past, T, R)
    ckv, krope = _mla_new(misc, cos2, sin2, p["kv_norm_g"], KVL=KVL, R=R, kr_block=KVL // LANES + 1)
    q_full = _mla_q(pm, cos2, sin2, p["q_norm_g"], HM=HM, NOPE=NOPE, R=R)
    kv_args = dict(HM=HM, NOPE=NOPE, R=R, V=V)
    _, kb_new, kb_cache = _attn_blocks(T, past)
    k_new, vt_new = _mla_kv(ckv[None], krope[None], 0, kb_new, p["w_ukv"], p["k_norm_g"], **kv_args)
    k_old, vt_old = (_mla_kv(ckv_cache, krope_cache, layer, kb_cache, p["w_ukv"], p["k_norm_g"], **kv_args)
                     if past else (None, None))
    y_b = _mla_attn(q_full, k_new, vt_new, k_old, vt_old, pm, HM=HM, V=V, g_block0=(HM * (NOPE + R)) // V)

    if past:
        k_cache = sb_k_cache.reshape(depth, b, past, HS * DS)
        v_cache = sb_v_cache.reshape(depth, b, past, HS * DS)
    else:
        k_cache = v_cache = None
    y_c = _sb_attn(psb, k_stack.reshape(depth, b, T, HS * DS), v_stack.reshape(depth, b, T, HS * DS),
                   k_cache, v_cache, layer, HS=HS, DS=DS)

    mixed = _merge(y_a.reshape(M, -1), y_b.reshape(M, -1), y_c.reshape(M, -1), p["w_branch"], gate_logit, p["b_merge"])
    h_new = _matmul(mixed, p["w_out"], residual=h.reshape(M, D), name="out_proj").reshape(b, T, D)
    return h_new, k_stack, v_stack, (ckv, krope, ssd_new, conv_new)


def kernel(x_prompt, x_sample, cache_sb_k, cache_sb_v, cache_mla_ckv, cache_mla_krope, state_ssd, state_conv,
           norm_g, w_in, conv_w, conv_b, dt_bias, a_log, d_skip, ssd_norm_g, mla_kv_norm_g, w_ukv,
           mla_q_norm_g, mla_k_norm_g, w_branch, b_merge, w_out):
    depth = w_in.shape[0]
    H, P, N = state_ssd.shape[2:]
    C = conv_w.shape[-1]
    W = H * P
    HS, DS = cache_sb_k.shape[3:]
    KVL = cache_mla_ckv.shape[-1]
    R = cache_mla_krope.shape[-1]
    NOPE = mla_q_norm_g.shape[-1] - R
    HM = (w_ukv.shape[-1] - w_branch.shape[2]) // NOPE
    V = w_ukv.shape[-1] // HM - NOPE
    dims = dict(H=H, P=P, N=N, C=C, W=W, G=(C - W) // (2 * N), HS=HS, DS=DS, KVL=KVL, R=R, NOPE=NOPE, HM=HM, V=V)
    assert KVL % LANES == 0 and HM * V == HS * DS == W

    hp, hs = x_prompt, x_sample
    kp = vp = ks = vs = None
    outs_p, outs_s = [], []
    for l in range(depth):
        p = _prep_layer(dims, norm_g[l], w_in[l], conv_w[l], conv_b[l], dt_bias[l], a_log[l], d_skip[l], ssd_norm_g[l],
                        mla_kv_norm_g[l], w_ukv[l], mla_q_norm_g[l], mla_k_norm_g[l], w_branch[l], b_merge[l], w_out[l])
        hp, kp, vp, st_p = _layer(dims, p, hp, l, depth, kp, vp, None, None, None, None, None, None)
        outs_p.append(st_p)
        hs, ks, vs, st_s = _layer(dims, p, hs, l, depth, ks, vs, cache_sb_k, cache_sb_v, cache_mla_ckv,
                                  cache_mla_krope, state_ssd[l], state_conv[l])
        outs_s.append(st_s)
    kv_shape = lambda a, h: a.reshape(depth, h.shape[0], h.shape[1], HS, DS)
    stack_p = [jnp.stack(s) for s in zip(*outs_p)]
    stack_s = [jnp.stack(s) for s in zip(*outs_s)]
    return (hp, hs, kv_shape(kp, hp), kv_shape(vp, hp), *stack_p, kv_shape(ks, hs), kv_shape(vs, hs), *stack_s)
```

```python
import functools
import math

import jax
import jax.numpy as jnp
from jax import lax
from jax.experimental import pallas as pl
from jax.experimental.pallas import tpu as pltpu

CHUNK = 64
Q_BLOCK = 128
EPS = 1e-6
ROPE_THETA = 10000.0
LANES = 128
NEG_BIG = -1e30
LOG2E = 1.4426950408889634
UNDERFLOW_LOG2 = 160.0
VMEM_LIMIT_BYTES = 48 * 1024 * 1024

F32 = jnp.float32
BF16 = jnp.bfloat16


def _params(*sem, flags=None):
    return pltpu.CompilerParams(dimension_semantics=sem, vmem_limit_bytes=VMEM_LIMIT_BYTES, flags=flags)


def _pick(n, prefs):
    for p in prefs:
        if n % p == 0:
            return p
    return n


def _sigmoid(x):
    return 1.0 / (1.0 + jnp.exp(-x))


def _silu(x):
    return x * _sigmoid(x)


def _softplus(x):
    return jnp.maximum(x, 0.0) + jnp.log(1.0 + jnp.exp(-jnp.abs(x)))


def _rmsnorm_kernel(x_ref, g_ref, o_ref):
    x = x_ref[...]
    ms = jnp.mean(x * x, axis=-1, keepdims=True)
    o_ref[...] = (x * lax.rsqrt(ms + EPS) * g_ref[...]).astype(o_ref.dtype)


def _rmsnorm(x2d, g):
    m, d = x2d.shape
    tm = _pick(m, (512, 256, 128, 64, 32, 16))
    return pl.pallas_call(
        _rmsnorm_kernel,
        grid=(m // tm,),
        in_specs=[pl.BlockSpec((tm, d), lambda i: (i, 0)), pl.BlockSpec((1, d), lambda i: (0, 0))],
        out_specs=pl.BlockSpec((tm, d), lambda i: (i, 0)),
        out_shape=jax.ShapeDtypeStruct((m, d), BF16),
        compiler_params=_params("parallel"),
        name="rmsnorm",
    )(x2d, g.reshape(1, d))


def _mm_kernel(x_ref, w_ref, o_ref):
    o_ref[...] = jnp.dot(x_ref[...], w_ref[...], preferred_element_type=F32).astype(o_ref.dtype)


def _mm_res_kernel(x_ref, w_ref, r_ref, o_ref):
    o_ref[...] = r_ref[...] + jnp.dot(x_ref[...], w_ref[...], preferred_element_type=F32)


def _mm_into_kernel(x_ref, w_ref, prev_ref, o_ref):
    del prev_ref
    _mm_kernel(x_ref, w_ref, o_ref)


def _mm_first_kernel(x_ref, w_ref, o_ref, *, layer):
    slab = pl.program_id(0)

    @pl.when(slab == layer)
    def _():
        _mm_kernel(x_ref, w_ref, o_ref)

    @pl.when(slab != layer)
    def _():
        o_ref[...] = jnp.zeros_like(o_ref)


def _matmul(x, w, residual=None, name="matmul", stack=None):
    m, k = x.shape
    n = w.shape[1]
    tm = _pick(m, (1024, 512, 256, 128, 64, 32, 16))
    tn = _pick(n, (1024, 512, 256, 128))
    in_specs = [pl.BlockSpec((tm, k), lambda i, j: (i, 0)), pl.BlockSpec((k, tn), lambda i, j: (0, j))]
    args = [x, w]
    kern = _mm_kernel
    out_spec = pl.BlockSpec((tm, tn), lambda i, j: (i, j))
    out_shape = jax.ShapeDtypeStruct((m, n), F32)
    aliases = {}
    if residual is not None:
        in_specs.append(pl.BlockSpec((tm, tn), lambda i, j: (i, j)))
        args.append(residual)
        kern = _mm_res_kernel
    grid = (m // tm, n // tn)
    sem = ("parallel", "arbitrary")
    if stack is not None:
        assert residual is None
        prev, layer, depth = stack
        out_shape = jax.ShapeDtypeStruct((depth, m, n), F32)
        if prev is not None:
            out_spec = pl.BlockSpec((None, tm, tn), lambda i, j: (layer, i, j))
            in_specs.append(pl.BlockSpec(memory_space=pl.ANY))
            args.append(prev)
            kern = _mm_into_kernel
            aliases = {2: 0}
        else:
            here = lambda s, i: jnp.where(s == layer, i, 0)
            in_specs = [pl.BlockSpec((tm, k), lambda s, i, j: (here(s, i), 0)),
                        pl.BlockSpec((k, tn), lambda s, i, j: (0, here(s, j)))]
            out_spec = pl.BlockSpec((None, tm, tn), lambda s, i, j: (s, i, j))
            kern = functools.partial(_mm_first_kernel, layer=layer)
            grid = (depth,) + grid
            sem = ("parallel",) + sem
    return pl.pallas_call(
        kern,
        grid=grid,
        in_specs=in_specs,
        out_specs=out_spec,
        out_shape=out_shape,
        input_output_aliases=aliases,
        compiler_params=_params(*sem),
        name=name,
    )(*args)


def _ssd_kernel(z_ref, xbc_ref, dt_ref, ctx_ref, h0_ref, cw_ref, cb_ref, dtb_ref, alog_ref, dskip_ref, ng_ref,
                y_ref, st_ref, pad_scr, y_scr, *, L, H, P, G, N, KW):
    c = pl.program_id(1)
    W = H * P
    hg = H // G
    gw = hg * P
    top = 8 - (KW - 1)

    @pl.when(c == 0)
    def _():
        pad_scr[top:8, :] = ctx_ref[0]
        st_ref[0] = h0_ref[0]

    pad_scr[8:8 + L, :] = xbc_ref[0]
    conv = cb_ref[...] + cw_ref[0:1, :] * pad_scr[top:top + L, :]
    for j in range(1, KW):
        conv = conv + cw_ref[j:j + 1, :] * pad_scr[top + j:top + j + L, :]
    pad_scr[top:8, :] = pad_scr[8 + L - (KW - 1):8 + L, :]
    conv = _silu(conv)
    xs = conv[:, :W]
    bm = conv[:, W:W + G * N]
    cm = conv[:, W + G * N:]

    dt = _softplus(dt_ref[0] + dtb_ref[...])
    a = -jnp.exp(alog_ref[...])
    da = dt * a
    row = lax.broadcasted_iota(jnp.int32, (L, L), 0)
    col = lax.broadcasted_iota(jnp.int32, (L, L), 1)
    tril = col <= row
    cum = jnp.dot(tril.astype(F32), da, precision=lax.Precision.HIGHEST, preferred_element_type=F32)
    cum_t = cum.T
    dt_t = dt.T
    cum_last = cum[L - 1:L, :]
    wend = jnp.exp(cum_last - cum) * dt
    ecum = jnp.exp(cum)
    cdec = jnp.exp(cum_last)
    left = lax.broadcasted_iota(jnp.int32, (L, LANES), 1) < P
    left1 = lax.broadcasted_iota(jnp.int32, (1, LANES), 1) < P

    for g in range(G):
        bg = bm[:, g * N:(g + 1) * N].astype(BF16)
        cg = cm[:, g * N:(g + 1) * N].astype(BF16)
        cbm = lax.dot_general(cg, bg, (((1,), (1,)), ((), ())), preferred_element_type=F32)
        st = st_ref[0, g]
        yoff = jnp.dot(cg, st.astype(BF16), preferred_element_type=F32)
        xw_parts, cd_parts = [], []
        for j in range(hg // 2):
            h1 = g * hg + 2 * j
            h2 = h1 + 1
            xp = xs[:, h1 * P:h1 * P + LANES]
            yp = None
            for h, keep_left in ((h1, True), (h2, False)):
                diff = cum[:, h:h + 1] - cum_t[h:h + 1, :]
                w = cbm * jnp.exp(jnp.where(tril, diff, NEG_BIG)) * dt_t[h:h + 1, :]
                xh = jnp.where(left, xp, 0.0) if keep_left else jnp.where(left, 0.0, xp)
                t = jnp.dot(w.astype(BF16), xh.astype(BF16), preferred_element_type=F32)
                yp = t if yp is None else yp + t
            e = jnp.where(left, ecum[:, h1:h1 + 1], ecum[:, h2:h2 + 1])
            y_scr[:, h1 * P:h1 * P + LANES] = yp + e * yoff[:, j * LANES:(j + 1) * LANES]
            we = jnp.where(left, wend[:, h1:h1 + 1], wend[:, h2:h2 + 1])
            xw_parts.append(xp * we)
            cd_parts.append(jnp.where(left1, cdec[:, h1:h1 + 1], cdec[:, h2:h2 + 1]))
        xw = jnp.concatenate(xw_parts, axis=1).astype(BF16)
        cd = jnp.concatenate(cd_parts, axis=1)
        upd = lax.dot_general(bg, xw, (((0,), (0,)), ((), ())), preferred_element_type=F32)
        st_ref[0, g] = cd * st + upd

    y = (y_scr[...] + dskip_ref[...] * xs) * _silu(z_ref[0])
    for g in range(G):
        yg = y[:, g * gw:(g + 1) * gw]
        ms = jnp.mean(yg * yg, axis=-1, keepdims=True)
        y_ref[0, :, g * gw:(g + 1) * gw] = (
            yg * lax.rsqrt(ms + EPS) * ng_ref[:, g * gw:(g + 1) * gw]).astype(y_ref.dtype)


def _ssd_branch(z, xbc, misc, dt_block, ctx, h0_t, cw, cb, dtb, alog, dskip_x, ng, *, H, P, G, N):
    b, T, W = z.shape
    C = xbc.shape[-1]
    KW = cw.shape[0]
    L = min(CHUNK, T)
    hg = H // G
    gw = hg * P
    assert T % L == 0 and L >= KW - 1 and hg % 2 == 0 and 2 * P == LANES and H <= LANES
    kern = functools.partial(_ssd_kernel, L=L, H=H, P=P, G=G, N=N, KW=KW)
    const2 = lambda bi, c: (0, 0)
    return pl.pallas_call(
        kern,
        grid=(b, T // L),
        in_specs=[
            pl.BlockSpec((1, L, W), lambda bi, c: (bi, c, 0)),
            pl.BlockSpec((1, L, C), lambda bi, c: (bi, c, 0)),
            pl.BlockSpec((1, L, LANES), lambda bi, c: (bi, c, dt_block)),
            pl.BlockSpec((1, KW - 1, C), lambda bi, c: (bi, 0, 0)),
            pl.BlockSpec((1, G, N, gw), lambda bi, c: (bi, 0, 0, 0)),
            pl.BlockSpec((KW, C), const2),
            pl.BlockSpec((1, C), const2),
            pl.BlockSpec((1, LANES), const2),
            pl.BlockSpec((1, LANES), const2),
            pl.BlockSpec((1, W), const2),
            pl.BlockSpec((1, W), const2),
        ],
        out_specs=[
            pl.BlockSpec((1, L, W), lambda bi, c: (bi, c, 0)),
            pl.BlockSpec((1, G, N, gw), lambda bi, c: (bi, 0, 0, 0)),
        ],
        out_shape=[jax.ShapeDtypeStruct((b, T, W), BF16), jax.ShapeDtypeStruct((b, G, N, gw), F32)],
        scratch_shapes=[pltpu.VMEM((8 + L, C), F32), pltpu.VMEM((L, W), F32)],
        compiler_params=_params("parallel", "arbitrary"),
        name="ssd_scan",
    )(z, xbc, misc, ctx, h0_t, cw, cb, dtb, alog, dskip_x, ng)


def _swap_halves(x, half):
    lane = lax.broadcasted_iota(jnp.int32, x.shape, 1)
    first = (lane % (2 * half)) < half
    return jnp.where(first, pltpu.roll(x, LANES - half, 1), pltpu.roll(x, half, 1))


def _mla_new_kernel(ckv_ref, kr_ref, cos_ref, sin_ref, g_ref, ckv_out, kr_out, *, R):
    x = ckv_ref[0]
    ms = jnp.mean(x * x, axis=-1, keepdims=True)
    ckv_out[0] = x * lax.rsqrt(ms + EPS) * g_ref[...]
    kr = kr_ref[0]
    rot = kr * cos_ref[...] + _swap_halves(kr, R // 2) * sin_ref[...]
    kr_out[0] = rot[:, :R]


def _mla_new(misc, cos2, sin2, kv_norm_g, *, KVL, R, kr_block):
    b, T, _ = misc.shape
    tb = _pick(T, (256, 128, 64))
    return pl.pallas_call(
        functools.partial(_mla_new_kernel, R=R),
        grid=(b, T // tb),
        in_specs=[
            pl.BlockSpec((1, tb, KVL), lambda bi, t: (bi, t, 0)),
            pl.BlockSpec((1, tb, LANES), lambda bi, t: (bi, t, kr_block)),
            pl.BlockSpec((tb, LANES), lambda bi, t: (t, 0)),
            pl.BlockSpec((tb, LANES), lambda bi, t: (t, 0)),
            pl.BlockSpec((1, KVL), lambda bi, t: (0, 0)),
        ],
        out_specs=[
            pl.BlockSpec((1, tb, KVL), lambda bi, t: (bi, t, 0)),
            pl.BlockSpec((1, tb, R), lambda bi, t: (bi, t, 0)),
        ],
        out_shape=[jax.ShapeDtypeStruct((b, T, KVL), F32), jax.ShapeDtypeStruct((b, T, R), F32)],
        compiler_params=_params("parallel", "parallel"),
        name="mla_new_kv",
    )(misc, misc, cos2, sin2, kv_norm_g.reshape(1, KVL))


def _mla_q_kernel(qn_ref, qr_ref, cos_ref, sin_ref, gn_ref, gr_ref, o_ref, *, HM, NOPE, R, scale):
    lane = lax.broadcasted_iota(jnp.int32, (qn_ref.shape[1], LANES), 1)
    left = lane < R
    for j in range(HM // 2):
        qr = qr_ref[0, :, j * LANES:(j + 1) * LANES]
        rot = qr * cos_ref[...] + _swap_halves(qr, R // 2) * sin_ref[...]
        sq = rot * rot
        for h, keep_left in ((2 * j, True), (2 * j + 1, False)):
            qn = qn_ref[0, :, h * NOPE:(h + 1) * NOPE]
            sel = left if keep_left else jnp.logical_not(left)
            ss = jnp.sum(qn * qn, axis=-1, keepdims=True) + jnp.sum(jnp.where(sel, sq, 0.0), axis=-1, keepdims=True)
            inv = lax.rsqrt(ss * (1.0 / (NOPE + R)) + EPS) * scale
            o_ref[0, :, h * 2 * LANES:h * 2 * LANES + NOPE] = (qn * inv * gn_ref[...]).astype(o_ref.dtype)
            o_ref[0, :, h * 2 * LANES + NOPE:(h + 1) * 2 * LANES] = (
                jnp.where(sel, rot * inv * gr_ref[...], 0.0)).astype(o_ref.dtype)


def _mla_q(pm, cos2, sin2, q_norm_g, *, HM, NOPE, R):
    b, T, _ = pm.shape
    assert NOPE == LANES and 2 * R == LANES and HM % 2 == 0
    tb = _pick(T, (256, 128, 64))
    scale = LOG2E / math.sqrt(NOPE + R)
    gn = q_norm_g[:NOPE].reshape(1, NOPE)
    gr = jnp.tile(q_norm_g[NOPE:], 2).reshape(1, LANES)
    nope_w, rope_w = HM * NOPE, HM * R
    assert nope_w % rope_w == 0
    return pl.pallas_call(
        functools.partial(_mla_q_kernel, HM=HM, NOPE=NOPE, R=R, scale=scale),
        grid=(b, T // tb),
        in_specs=[
            pl.BlockSpec((1, tb, nope_w), lambda bi, t: (bi, t, 0)),
            pl.BlockSpec((1, tb, rope_w), lambda bi, t: (bi, t, nope_w // rope_w)),
            pl.BlockSpec((tb, LANES), lambda bi, t: (t, 0)),
            pl.BlockSpec((tb, LANES), lambda bi, t: (t, 0)),
            pl.BlockSpec((1, NOPE), lambda bi, t: (0, 0)),
            pl.BlockSpec((1, LANES), lambda bi, t: (0, 0)),
        ],
        out_specs=pl.BlockSpec((1, tb, HM * 2 * LANES), lambda bi, t: (bi, t, 0)),
        out_shape=jax.ShapeDtypeStruct((b, T, HM * 2 * LANES), BF16),
        compiler_params=_params("parallel", "parallel"),
        name="mla_q_prep",
    )(pm, pm, cos2, sin2, gn, gr)


def _mla_kv_kernel(ckv_ref, kr_ref, w_ref, gn_ref, gr_ref, k_ref, vt_ref, *, HM, NOPE, R, V):
    ckv = ckv_ref[0, 0].astype(BF16)
    kv = jnp.dot(ckv, w_ref[...], preferred_element_type=F32)
    kr = kr_ref[0, 0]
    ss_r = jnp.sum(kr * kr, axis=-1, keepdims=True)
    krg = kr * gr_ref[...]
    zero = jnp.zeros_like(krg)
    kr_even = jnp.concatenate([krg, zero], axis=1)
    kr_odd = jnp.concatenate([zero, krg], axis=1)
    for h in range(HM):
        base = h * (NOPE + V)
        kn = kv[:, base:base + NOPE]
        ss = jnp.sum(kn * kn, axis=-1, keepdims=True) + ss_r
        inv = lax.rsqrt(ss * (1.0 / (NOPE + R)) + EPS)
        k_ref[0, :, h * 2 * LANES:h * 2 * LANES + NOPE] = (kn * inv * gn_ref[...]).astype(k_ref.dtype)
        k_ref[0, :, h * 2 * LANES + NOPE:(h + 1) * 2 * LANES] = (
            (kr_even if h % 2 == 0 else kr_odd) * inv).astype(k_ref.dtype)
        vt_ref[0, 0, h * V:(h + 1) * V, :] = kv[:, base + NOPE:base + NOPE + V].T.astype(vt_ref.dtype)


def _mla_kv(ckv, krope, layer, ts, w_ukv, k_norm_g, *, HM, NOPE, R, V):
    _, b, S, KVL = ckv.shape
    assert S % ts == 0
    gn = k_norm_g[:NOPE].reshape(1, NOPE)
    gr = k_norm_g[NOPE:].reshape(1, R)
    return pl.pallas_call(
        functools.partial(_mla_kv_kernel, HM=HM, NOPE=NOPE, R=R, V=V),
        grid=(b, S // ts),
        in_specs=[
            pl.BlockSpec((1, 1, ts, KVL), lambda bi, s: (layer, bi, s, 0)),
            pl.BlockSpec((1, 1, ts, R), lambda bi, s: (layer, bi, s, 0)),
            pl.BlockSpec((KVL, HM * (NOPE + V)), lambda bi, s: (0, 0)),
            pl.BlockSpec((1, NOPE), lambda bi, s: (0, 0)),
            pl.BlockSpec((1, R), lambda bi, s: (0, 0)),
        ],
        out_specs=[
            pl.BlockSpec((1, ts, HM * 2 * LANES), lambda bi, s: (bi, s, 0)),
            pl.BlockSpec((1, 1, HM * V, ts), lambda bi, s: (bi, s, 0, 0)),
        ],
        out_shape=[jax.ShapeDtypeStruct((b, S, HM * 2 * LANES), BF16),
                   jax.ShapeDtypeStruct((b, S // ts, HM * V, ts), BF16)],
        compiler_params=_params("parallel", "parallel"),
        name="mla_kv_prep",
    )(ckv, krope, w_ukv, gn, gr)


def _attn_blocks(T, past):
    qb = min(T, 4 * Q_BLOCK)
    kb = min(qb, 2 * Q_BLOCK)
    kbc = _pick(past, (2 * Q_BLOCK, Q_BLOCK, CHUNK)) if past else kb
    assert T % qb == 0 and qb % kb == 0 and kb % CHUNK == 0 and past % CHUNK == 0 and past % kbc == 0
    assert T % Q_BLOCK == 0 or T <= Q_BLOCK
    return qb, kb, kbc


def _mla_scores(q, k):
    return lax.dot_general(k, q, (((1,), (1,)), ((), ())), preferred_element_type=F32)


def _mla_tile(q, k, vt, m, l, acc, mask):
    s = _mla_scores(q, k)
    if mask is not None:
        s = jnp.where(mask, s, NEG_BIG)
    return _mla_accumulate(s, vt, m, l, acc)


def _mla_accumulate(s, vt, m, l, acc):
    m_new = jnp.maximum(m, jnp.max(s, axis=0, keepdims=True))
    p = jnp.exp2(s - m_new)
    alpha = jnp.exp2(m - m_new)
    l = alpha * l + jnp.sum(p, axis=0, keepdims=True)
    acc = alpha * acc + jnp.dot(vt, p.astype(BF16), preferred_element_type=F32)
    return m_new, l, acc


def _mla_attn_kernel(*refs, QB, KB, KBC, n_cache, single, shift):
    if n_cache:
        q_ref, kn_ref, vtn_ref, kc_ref, vtc_ref, g_ref, o_ref = refs
    else:
        q_ref, kn_ref, vtn_ref, g_ref, o_ref = refs
    qi = pl.program_id(2)
    q = q_ref[0]
    dv = vtn_ref.shape[2]
    m = jnp.full((1, QB), NEG_BIG, F32)
    l = jnp.zeros((1, QB), F32)
    acc = jnp.zeros((dv, QB), F32)
    nd = QB // KB
    for kk in range(nd):
        r0 = kk * KB
        rows = QB - r0
        start = r0 if single else pl.multiple_of(qi * QB + r0, KB)
        k = kn_ref[0, pl.ds(start, KB), :]
        vt = vtn_ref[0, kk if single else qi * nd + kk]
        kchunk = lax.shift_right_logical(lax.broadcasted_iota(jnp.int32, (KB, rows), 0), shift)
        rchunk = lax.shift_right_logical(lax.broadcasted_iota(jnp.int32, (KB, rows), 1), shift)
        mn, ln, an = _mla_tile(q[r0:], k, vt, m[:, r0:], l[:, r0:], acc[:, r0:], kchunk <= rchunk)
        if r0:
            mn, ln, an = (jnp.concatenate([a[:, :r0], b], axis=1) for a, b in ((m, mn), (l, ln), (acc, an)))
        m, l, acc = mn, ln, an

    def sweep(carry, k_ref, vt_ref, n, per_tile, kb):
        width = per_tile * kb

        def scores(j):
            return _mla_scores(q, k_ref[0, pl.ds(pl.multiple_of(j * width, width), width), :])

        def values(j):
            return jnp.concatenate([vt_ref[0, j * per_tile + u] for u in range(per_tile)], axis=1)

        if not isinstance(n, int):
            return lax.fori_loop(0, n, lambda j, c: _mla_accumulate(scores(j), values(j), *c), carry)

        def body(j, state):
            s_next = scores(j + 1)
            return (s_next,) + _mla_accumulate(state[0], values(j), *state[1:])

        state = lax.fori_loop(0, n - 1, body, (scores(0),) + carry)
        return _mla_accumulate(state[0], values(n - 1), *state[1:])

    if not single:
        per_new = 2 if nd % 2 == 0 else 1
        m, l, acc = sweep((m, l, acc), kn_ref, vtn_ref, (qi * nd) // per_new, per_new, KB)
    if n_cache:
        per_cache = 2 if n_cache % 2 == 0 else 1
        m, l, acc = sweep((m, l, acc), kc_ref, vtc_ref, n_cache // per_cache, per_cache, KBC)
    o_ref[0] = ((acc * (1.0 / l)).T * _silu(g_ref[0])).astype(o_ref.dtype)


def _mla_attn(q_full, k_new, vt_new, k_cache, vt_cache, pm, *, HM, V, g_block0):
    b, T, _ = q_full.shape
    assert CHUNK & (CHUNK - 1) == 0 and V == LANES
    past = 0 if k_cache is None else k_cache.shape[1]
    QB, KB, KBC = _attn_blocks(T, past)
    assert vt_new.shape[-1] == KB and (not past or vt_cache.shape[-1] == KBC)
    kern = functools.partial(_mla_attn_kernel, QB=QB, KB=KB, KBC=KBC, n_cache=past // KBC, single=T == QB,
                             shift=CHUNK.bit_length() - 1)
    in_specs = [
        pl.BlockSpec((1, QB, 2 * LANES), lambda bi, h, t: (bi, t, h)),
        pl.BlockSpec((1, T, 2 * LANES), lambda bi, h, t: (bi, 0, h)),
        pl.BlockSpec((1, T // KB, V, KB), lambda bi, h, t: (bi, 0, h, 0)),
    ]
    args = [q_full, k_new, vt_new]
    if past:
        in_specs += [pl.BlockSpec((1, past, 2 * LANES), lambda bi, h, t: (bi, 0, h)),
                     pl.BlockSpec((1, past // KBC, V, KBC), lambda bi, h, t: (bi, 0, h, 0))]
        args += [k_cache, vt_cache]
    in_specs.append(pl.BlockSpec((1, QB, V), lambda bi, h, t: (bi, t, g_block0 + h)))
    args.append(pm)
    return pl.pallas_call(
        kern,
        grid=(b, HM, T // QB),
        in_specs=in_specs,
        out_specs=pl.BlockSpec((1, QB, V), lambda bi, h, t: (bi, t, h)),
        out_shape=jax.ShapeDtypeStruct((b, T, HM * V), BF16),
        compiler_params=_params("parallel", "parallel", "arbitrary"),
        name="mla_attention",
    )(*args)


def _strict_lower(n):
    row = lax.broadcasted_iota(jnp.int32, (2 * n, n), 0)
    col = lax.broadcasted_iota(jnp.int32, (2 * n, n), 1)
    return (jnp.where(row >= n, row - n, row) > col).astype(BF16)


def _sb_tile(q, k, v, run, acc, tri2, causal):
    z = lax.dot_general(q, k, (((1,), (1,)), ((), ())), preferred_element_type=F32)
    neg_abs = lax.bitcast_convert_type(lax.bitcast_convert_type(z, jnp.uint32) | jnp.uint32(0x80000000), F32)
    sp = jnp.log2(1.0 + jnp.exp2(neg_abs)) + jnp.maximum(z, 0.0)
    logsig = z - sp
    if causal:
        mask = lax.broadcasted_iota(jnp.int32, z.shape, 1) < lax.broadcasted_iota(jnp.int32, z.shape, 0)
        sp = jnp.where(mask, sp, 0.0)
    hi = sp.astype(BF16)
    lo = (sp - hi.astype(F32)).astype(BF16)
    later = jnp.dot(jnp.concatenate([hi, lo], axis=1), tri2, preferred_element_type=F32)
    att = jnp.exp2(logsig - later - run)
    if causal:
        att = jnp.where(mask, att, 0.0)
    acc = acc + jnp.dot(att.astype(BF16), v, preferred_element_type=F32)
    run = run + jnp.sum(sp, axis=-1, keepdims=True)
    return run, acc


def _sb_attn_kernel(*refs, QB, KB, KBC, n_cache, layer, single, scale):
    if n_cache:
        q_ref, kn_ref, vn_ref, kc_hbm, vc_hbm, g_ref, o_ref, kbuf, vbuf, sem = refs
        bi, head = pl.program_id(0), pl.program_id(1)

        def cache_copies(i):
            start = pl.multiple_of((n_cache - 1 - i) * KBC, KBC)
            slot = i % 2
            return (pltpu.make_async_copy(kc_hbm.at[layer, bi, pl.ds(start, KBC), head, :], kbuf.at[slot], sem.at[0, slot]),
                    pltpu.make_async_copy(vc_hbm.at[layer, bi, pl.ds(start, KBC), head, :], vbuf.at[slot], sem.at[1, slot]))

        for c in cache_copies(0):
            c.start()
    else:
        q_ref, kn_ref, vn_ref, g_ref, o_ref = refs
    qi = pl.program_id(2)
    q = (q_ref[0] * scale).astype(BF16)
    d = q.shape[-1]
    tri = _strict_lower(KB)
    run = jnp.zeros((QB, 1), F32)
    acc = jnp.zeros((QB, d), F32)
    nd = QB // KB
    for kk in reversed(range(nd)):
        r0 = kk * KB
        start = r0 if single else pl.multiple_of(qi * QB + r0, KB)
        k = kn_ref[0, 0, pl.ds(start, KB), :].astype(BF16)
        v = vn_ref[0, 0, pl.ds(start, KB), :].astype(BF16)
        rn, an = _sb_tile(q[r0:], k, v, run[r0:], acc[r0:], tri, True)
        if r0:
            rn, an = jnp.concatenate([run[:r0], rn], axis=0), jnp.concatenate([acc[:r0], an], axis=0)
        run, acc = rn, an

    def sweep(carry, n, load, tri_kb):
        def cond(state):
            i, run, _ = state
            return jnp.logical_and(i < n, jnp.min(run) < UNDERFLOW_LOG2)

        def body(state):
            i, run, acc = state
            k, v = load(i)
            return (i + 1,) + _sb_tile(q, k.astype(BF16), v.astype(BF16), run, acc, tri_kb, False)

        return lax.while_loop(cond, body, (jnp.int32(0),) + carry)

    n_new = qi * nd

    def load_new(i):
        start = pl.multiple_of((n_new - 1 - i) * KB, KB)
        return kn_ref[0, 0, pl.ds(start, KB), :], vn_ref[0, 0, pl.ds(start, KB), :]

    _, run, acc = sweep((run, acc), n_new, load_new, tri)
    if n_cache:
        def load_cache(i):
            for c in cache_copies(i):
                c.wait()

            @pl.when(i + 1 < n_cache)
            def _():
                for c in cache_copies(i + 1):
                    c.start()

            return kbuf[i % 2], vbuf[i % 2]

        used, run, acc = sweep((run, acc), n_cache, load_cache, tri if KBC == KB else _strict_lower(KBC))

        @pl.when(used < n_cache)
        def _():
            for c in cache_copies(used):
                c.wait()

    o_ref[0] = (acc * _silu(g_ref[0])).astype(o_ref.dtype)


def _sb_attn(psb, k_new, v_new, k_cache, v_cache, layer, *, HS, DS):
    b, T, _ = psb.shape
    assert DS == LANES
    past = 0 if k_cache is None else k_cache.shape[2]
    QB, KB, KBC = _attn_blocks(T, past)
    kern = functools.partial(_sb_attn_kernel, QB=QB, KB=KB, KBC=KBC, n_cache=past // KBC, layer=layer,
                             single=T == QB, scale=LOG2E / math.sqrt(DS))
    in_specs = [
        pl.BlockSpec((1, QB, DS), lambda bi, h, t: (bi, t, h)),
        pl.BlockSpec((1, 1, T, DS), lambda bi, h, t: (layer, bi, 0, h)),
        pl.BlockSpec((1, 1, T, DS), lambda bi, h, t: (layer, bi, 0, h)),
    ]
    args = [psb, k_new, v_new]
    scratch = []
    if past:
        in_specs += [pl.BlockSpec(memory_space=pl.ANY)] * 2
        args += [k_cache, v_cache]
        scratch = [pltpu.VMEM((2, KBC, DS), k_cache.dtype), pltpu.VMEM((2, KBC, DS), v_cache.dtype),
                   pltpu.SemaphoreType.DMA((2, 2))]
    in_specs.append(pl.BlockSpec((1, QB, DS), lambda bi, h, t: (bi, t, HS + h)))
    args.append(psb)
    return pl.pallas_call(
        kern,
        grid=(b, HS, T // QB),
        in_specs=in_specs,
        out_specs=pl.BlockSpec((1, QB, DS), lambda bi, h, t: (bi, t, h)),
        out_shape=jax.ShapeDtypeStruct((b, T, HS * DS), BF16),
        scratch_shapes=scratch,
        compiler_params=_params("parallel", "parallel", "arbitrary"),
        name="sb_attention",
    )(*args)


def _merge_kernel(ya_ref, yb_ref, yc_ref, w_ref, g0_ref, g1_ref, g2_ref, bm_ref, o_ref):
    acc = None
    for i, (y_ref, g_ref) in enumerate(((ya_ref, g0_ref), (yb_ref, g1_ref), (yc_ref, g2_ref))):
        gate = _sigmoid(g_ref[...] + bm_ref[i])
        t = gate * jnp.dot(y_ref[...], w_ref[i], preferred_element_type=F32)
        acc = t if acc is None else acc + t
    o_ref[...] = acc.astype(o_ref.dtype)


def _merge(ya, yb, yc, w_branch, gate_logit, b_merge):
    m, k = ya.shape
    nb, _, d = w_branch.shape
    tm = _pick(m, (512, 256, 128, 64, 32, 16))
    tn = _pick(d, (512, 256, 128))
    nt = d // tn
    yspec = pl.BlockSpec((tm, k), lambda i, j: (i, 0))
    gspecs = [pl.BlockSpec((tm, tn), functools.partial(lambda i, j, br: (i, br * nt + j), br=br)) for br in range(nb)]
    return pl.pallas_call(
        _merge_kernel,
        grid=(m // tm, nt),
        in_specs=[yspec, yspec, yspec, pl.BlockSpec((nb, k, tn), lambda i, j: (0, 0, j))] + gspecs
        + [pl.BlockSpec((nb, 1, tn), lambda i, j: (0, 0, j))],
        out_specs=pl.BlockSpec((tm, tn), lambda i, j: (i, j)),
        out_shape=jax.ShapeDtypeStruct((m, d), BF16),
        compiler_params=_params("parallel", "arbitrary"),
        name="branch_merge",
    )(ya, yb, yc, w_branch, gate_logit, gate_logit, gate_logit, b_merge.reshape(nb, 1, d))


def _rope_tables(past, T, R):
    half = R // 2
    freq = ROPE_THETA ** (-jnp.arange(half, dtype=F32) / half)
    ang = (past + jnp.arange(T)).astype(F32)[:, None] * freq[None, :]
    cos, sin = jnp.cos(ang), jnp.sin(ang)
    reps = LANES // R
    return jnp.tile(jnp.concatenate([cos, cos], axis=1), (1, reps)), jnp.tile(jnp.concatenate([-sin, sin], axis=1), (1, reps))


def _prep_layer(dims, norm_g, w_in, conv_w, conv_b, dt_bias, a_log, d_skip, ssd_norm_g, kv_norm_g, w_ukv,
                q_norm_g, k_norm_g, w_branch, b_merge, w_out):
    D = w_in.shape[0]
    W, C, H, P = dims["W"], dims["C"], dims["H"], dims["P"]
    HM, NOPE, R, KVL, HS, DS = dims["HM"], dims["NOPE"], dims["R"], dims["KVL"], dims["HS"], dims["DS"]
    sizes = (W, C, H, HM * (NOPE + R), KVL, R, HM * dims["V"], HS * DS, HS * DS, HS * DS, HS * DS, w_branch.shape[0] * D)
    offs = [0]
    for s in sizes:
        offs.append(offs[-1] + s)
    col = lambda i: w_in[:, offs[i]:offs[i + 1]]
    pad_to = lambda w, n: jnp.pad(w, ((0, 0), (0, n - w.shape[1])))
    wq = col(3).reshape(D, HM, NOPE + R)
    w_mla = jnp.concatenate([wq[:, :, :NOPE].reshape(D, HM * NOPE), wq[:, :, NOPE:].reshape(D, HM * R), col(6)], axis=1)
    w_misc = jnp.concatenate([col(4), pad_to(col(2), LANES), pad_to(col(5), LANES)], axis=1)
    vec = lambda v: jnp.pad(v, (0, LANES - v.shape[0])).reshape(1, LANES)
    return dict(
        norm_g=norm_g,
        w_z=col(0).astype(BF16), w_xbc=col(1).astype(BF16), w_misc=w_misc.astype(BF16), w_mla=w_mla.astype(BF16),
        w_sb=jnp.concatenate([col(7), col(10)], axis=1).astype(BF16),
        w_k=col(8).astype(BF16), w_v=col(9).astype(BF16), w_gate=col(11).astype(BF16),
        conv_w=conv_w, conv_b=conv_b.reshape(1, C), dt_bias=vec(dt_bias), a_log=vec(a_log),
        dskip_x=jnp.repeat(d_skip, P).reshape(1, W), ssd_norm_g=ssd_norm_g.reshape(1, W),
        kv_norm_g=kv_norm_g, w_ukv=w_ukv.astype(BF16), q_norm_g=q_norm_g, k_norm_g=k_norm_g,
        w_branch=w_branch.astype(BF16), b_merge=b_merge, w_out=w_out.astype(BF16),
    )


def _layer(dims, p, h, layer, depth, k_stack, v_stack, sb_k_cache, sb_v_cache, ckv_cache, krope_cache, ssd_h0, conv_ctx):
    b, T, D = h.shape
    M = b * T
    H, P, G, N = dims["H"], dims["P"], dims["G"], dims["N"]
    HM, NOPE, R, V, KVL, HS, DS = dims["HM"], dims["NOPE"], dims["R"], dims["V"], dims["KVL"], dims["HS"], dims["DS"]
    hg = H // G
    past = 0 if sb_k_cache is None else sb_k_cache.shape[2]

    u = _rmsnorm(h.reshape(M, D), p["norm_g"])
    z = _matmul(u, p["w_z"], name="proj_z").reshape(b, T, -1)
    xbc = _matmul(u, p["w_xbc"], name="proj_xbc").reshape(b, T, -1)
    misc = _matmul(u, p["w_misc"], name="proj_misc").reshape(b, T, -1)
    pm = _matmul(u, p["w_mla"], name="proj_mla").reshape(b, T, -1)
    psb = _matmul(u, p["w_sb"], name="proj_sb").reshape(b, T, -1)
    k_stack = _matmul(u, p["w_k"], name="proj_sbk", stack=(k_stack, layer, depth))
    v_stack = _matmul(u, p["w_v"], name="proj_sbv", stack=(v_stack, layer, depth))
    gate_logit = _matmul(u, p["w_gate"], name="proj_gate")

    KW = p["conv_w"].shape[0]
    assert T >= KW - 1
    if ssd_h0 is None:
        h0_t = jnp.zeros((b, G, N, hg * P), F32)
        ctx = jnp.zeros((b, KW - 1, xbc.shape[-1]), F32)
    else:
        h0_t = jnp.swapaxes(ssd_h0.reshape(b, G, hg * P, N), 2, 3)
        ctx = conv_ctx
    y_a, st_t = _ssd_branch(z, xbc, misc, KVL // LANES, ctx, h0_t, p["conv_w"], p["conv_b"], p["dt_bias"],
                            p["a_log"], p["dskip_x"], p["ssd_norm_g"], H=H, P=P, G=G, N=N)
    ssd_new = jnp.swapaxes(st_t, 2, 3).reshape(b, H, P, N)
    conv_new = xbc[:, T - (KW - 1):, :]

    cos2, sin2 = _rope_tables(past, T, R)
    ckv, krope = _mla_new(misc, cos2, sin2, p["kv_norm_g"], KVL=KVL, R=R, kr_block=KVL // LANES + 1)
    q_full = _mla_q(pm, cos2, sin2, p["q_norm_g"], HM=HM, NOPE=NOPE, R=R)
    kv_args = dict(HM=HM, NOPE=NOPE, R=R, V=V)
    _, kb_new, kb_cache = _attn_blocks(T, past)
    k_new, vt_new = _mla_kv(ckv[None], krope[None], 0, kb_new, p["w_ukv"], p["k_norm_g"], **kv_args)
    k_old, vt_old = (_mla_kv(ckv_cache, krope_cache, layer, kb_cache, p["w_ukv"], p["k_norm_g"], **kv_args)
                     if past else (None, None))
    y_b = _mla_attn(q_full, k_new, vt_new, k_old, vt_old, pm, HM=HM, V=V, g_block0=(HM * (NOPE + R)) // V)

    y_c = _sb_attn(psb, k_stack.reshape(depth, b, T, HS * DS), v_stack.reshape(depth, b, T, HS * DS),
                   sb_k_cache, sb_v_cache, layer, HS=HS, DS=DS)

    mixed = _merge(y_a.reshape(M, -1), y_b.reshape(M, -1), y_c.reshape(M, -1), p["w_branch"], gate_logit, p["b_merge"])
    h_new = _matmul(mixed, p["w_out"], residual=h.reshape(M, D), name="out_proj").reshape(b, T, D)
    return h_new, k_stack, v_stack, (ckv, krope, ssd_new, conv_new)


def kernel(x_prompt, x_sample, cache_sb_k, cache_sb_v, cache_mla_ckv, cache_mla_krope, state_ssd, state_conv,
           norm_g, w_in, conv_w, conv_b, dt_bias, a_log, d_skip, ssd_norm_g, mla_kv_norm_g, w_ukv,
           mla_q_norm_g, mla_k_norm_g, w_branch, b_merge, w_out):
    depth = w_in.shape[0]
    H, P, N = state_ssd.shape[2:]
    C = conv_w.shape[-1]
    W = H * P
    HS, DS = cache_sb_k.shape[3:]
    KVL = cache_mla_ckv.shape[-1]
    R = cache_mla_krope.shape[-1]
    NOPE = mla_q_norm_g.shape[-1] - R
    HM = (w_ukv.shape[-1] - w_branch.shape[2]) // NOPE
    V = w_ukv.shape[-1] // HM - NOPE
    dims = dict(H=H, P=P, N=N, C=C, W=W, G=(C - W) // (2 * N), HS=HS, DS=DS, KVL=KVL, R=R, NOPE=NOPE, HM=HM, V=V)
    assert KVL % LANES == 0 and HM * V == HS * DS == W

    hp, hs = x_prompt, x_sample
    kp = vp = ks = vs = None
    outs_p, outs_s = [], []
    for l in range(depth):
        p = _prep_layer(dims, norm_g[l], w_in[l], conv_w[l], conv_b[l], dt_bias[l], a_log[l], d_skip[l], ssd_norm_g[l],
                        mla_kv_norm_g[l], w_ukv[l], mla_q_norm_g[l], mla_k_norm_g[l], w_branch[l], b_merge[l], w_out[l])
        hp, kp, vp, st_p = _layer(dims, p, hp, l, depth, kp, vp, None, None, None, None, None, None)
        outs_p.append(st_p)
        hs, ks, vs, st_s = _layer(dims, p, hs, l, depth, ks, vs, cache_sb_k, cache_sb_v, cache_mla_ckv,
                                  cache_mla_krope, state_ssd[l], state_conv[l])
        outs_s.append(st_s)
    kv_shape = lambda a, h: a.reshape(depth, h.shape[0], h.shape[1], HS, DS)
    stack_p = [jnp.stack(s) for s in zip(*outs_p)]
    stack_s = [jnp.stack(s) for s in zip(*outs_s)]
    return (hp, hs, kv_shape(kp, hp), kv_shape(vp, hp), *stack_p, kv_shape(ks, hs), kv_shape(vs, hs), *stack_s)
```

```python
import functools
import math

import jax
import jax.numpy as jnp
from jax import lax
from jax.experimental import pallas as pl
from jax.experimental.pallas import tpu as pltpu

CHUNK = 64
Q_BLOCK = 128
EPS = 1e-6
ROPE_THETA = 10000.0
LANES = 128
NEG_BIG = -1e30
LOG2E = 1.4426950408889634
UNDERFLOW_LOG2 = 160.0
VMEM_LIMIT_BYTES = 48 * 1024 * 1024

F32 = jnp.float32
BF16 = jnp.bfloat16


def _params(*sem, flags=None):
    return pltpu.CompilerParams(dimension_semantics=sem, vmem_limit_bytes=VMEM_LIMIT_BYTES, flags=flags)


def _pick(n, prefs):
    for p in prefs:
        if n % p == 0:
            return p
    return n


def _sigmoid(x):
    return 1.0 / (1.0 + jnp.exp(-x))


def _silu(x):
    return x * _sigmoid(x)


def _softplus(x):
    return jnp.maximum(x, 0.0) + jnp.log(1.0 + jnp.exp(-jnp.abs(x)))


def _rmsnorm_kernel(x_ref, g_ref, o_ref):
    x = x_ref[...]
    ms = jnp.mean(x * x, axis=-1, keepdims=True)
    o_ref[...] = (x * lax.rsqrt(ms + EPS) * g_ref[...]).astype(o_ref.dtype)


def _rmsnorm(x2d, g):
    m, d = x2d.shape
    tm = _pick(m, (512, 256, 128, 64, 32, 16))
    return pl.pallas_call(
        _rmsnorm_kernel,
        grid=(m // tm,),
        in_specs=[pl.BlockSpec((tm, d), lambda i: (i, 0)), pl.BlockSpec((1, d), lambda i: (0, 0))],
        out_specs=pl.BlockSpec((tm, d), lambda i: (i, 0)),
        out_shape=jax.ShapeDtypeStruct((m, d), BF16),
        compiler_params=_params("parallel"),
        name="rmsnorm",
    )(x2d, g.reshape(1, d))


def _mm_kernel(x_ref, w_ref, o_ref):
    o_ref[...] = jnp.dot(x_ref[...], w_ref[...], preferred_element_type=F32).astype(o_ref.dtype)


def _mm_res_kernel(x_ref, w_ref, r_ref, o_ref):
    o_ref[...] = r_ref[...] + jnp.dot(x_ref[...], w_ref[...], preferred_element_type=F32)


def _mm_into_kernel(x_ref, w_ref, prev_ref, o_ref):
    del prev_ref
    _mm_kernel(x_ref, w_ref, o_ref)


def _mm_first_kernel(x_ref, w_ref, o_ref, *, layer):
    slab = pl.program_id(0)

    @pl.when(slab == layer)
    def _():
        _mm_kernel(x_ref, w_ref, o_ref)

    @pl.when(slab != layer)
    def _():
        o_ref[...] = jnp.zeros_like(o_ref)


def _matmul(x, w, residual=None, name="matmul", stack=None):
    m, k = x.shape
    n = w.shape[1]
    tm = _pick(m, (1024, 512, 256, 128, 64, 32, 16))
    tn = _pick(n, (1024, 512, 256, 128))
    in_specs = [pl.BlockSpec((tm, k), lambda i, j: (i, 0)), pl.BlockSpec((k, tn), lambda i, j: (0, j))]
    args = [x, w]
    kern = _mm_kernel
    out_spec = pl.BlockSpec((tm, tn), lambda i, j: (i, j))
    out_shape = jax.ShapeDtypeStruct((m, n), F32)
    aliases = {}
    if residual is not None:
        in_specs.append(pl.BlockSpec((tm, tn), lambda i, j: (i, j)))
        args.append(residual)
        kern = _mm_res_kernel
    grid = (m // tm, n // tn)
    sem = ("parallel", "arbitrary")
    if stack is not None:
        assert residual is None
        prev, layer, depth = stack
        out_shape = jax.ShapeDtypeStruct((depth, m, n), F32)
        if prev is not None:
            out_spec = pl.BlockSpec((None, tm, tn), lambda i, j: (layer, i, j))
            in_specs.append(pl.BlockSpec(memory_space=pl.ANY))
            args.append(prev)
            kern = _mm_into_kernel
            aliases = {2: 0}
        else:
            here = lambda s, i: jnp.where(s == layer, i, 0)
            in_specs = [pl.BlockSpec((tm, k), lambda s, i, j: (here(s, i), 0)),
                        pl.BlockSpec((k, tn), lambda s, i, j: (0, here(s, j)))]
            out_spec = pl.BlockSpec((None, tm, tn), lambda s, i, j: (s, i, j))
            kern = functools.partial(_mm_first_kernel, layer=layer)
            grid = (depth,) + grid
            sem = ("parallel",) + sem
    return pl.pallas_call(
        kern,
        grid=grid,
        in_specs=in_specs,
        out_specs=out_spec,
        out_shape=out_shape,
        input_output_aliases=aliases,
        compiler_params=_params(*sem),
        name=name,
    )(*args)


def _ssd_kernel(z_ref, xbc_ref, dt_ref, ctx_ref, h0_ref, cw_ref, cb_ref, dtb_ref, alog_ref, dskip_ref, ng_ref,
                y_ref, st_ref, pad_scr, y_scr, *, L, H, P, G, N, KW):
    c = pl.program_id(1)
    W = H * P
    hg = H // G
    gw = hg * P
    top = 8 - (KW - 1)

    @pl.when(c == 0)
    def _():
        pad_scr[top:8, :] = ctx_ref[0]
        st_ref[0] = h0_ref[0]

    pad_scr[8:8 + L, :] = xbc_ref[0]
    conv = cb_ref[...] + cw_ref[0:1, :] * pad_scr[top:top + L, :]
    for j in range(1, KW):
        conv = conv + cw_ref[j:j + 1, :] * pad_scr[top + j:top + j + L, :]
    pad_scr[top:8, :] = pad_scr[8 + L - (KW - 1):8 + L, :]
    conv = _silu(conv)
    xs = conv[:, :W]
    bm = conv[:, W:W + G * N]
    cm = conv[:, W + G * N:]

    dt = _softplus(dt_ref[0] + dtb_ref[...])
    a = -jnp.exp(alog_ref[...])
    da = dt * a
    row = lax.broadcasted_iota(jnp.int32, (L, L), 0)
    col = lax.broadcasted_iota(jnp.int32, (L, L), 1)
    tril = col <= row
    cum = jnp.dot(tril.astype(F32), da, precision=lax.Precision.HIGHEST, preferred_element_type=F32)
    cum_t = cum.T
    dt_t = dt.T
    cum_last = cum[L - 1:L, :]
    wend = jnp.exp(cum_last - cum) * dt
    ecum = jnp.exp(cum)
    cdec = jnp.exp(cum_last)
    left = lax.broadcasted_iota(jnp.int32, (L, LANES), 1) < P
    left1 = lax.broadcasted_iota(jnp.int32, (1, LANES), 1) < P

    for g in range(G):
        bg = bm[:, g * N:(g + 1) * N].astype(BF16)
        cg = cm[:, g * N:(g + 1) * N].astype(BF16)
        cbm = lax.dot_general(cg, bg, (((1,), (1,)), ((), ())), preferred_element_type=F32)
        st = st_ref[0, g]
        yoff = jnp.dot(cg, st.astype(BF16), preferred_element_type=F32)
        xw_parts, cd_parts = [], []
        for j in range(hg // 2):
            h1 = g * hg + 2 * j
            h2 = h1 + 1
            xp = xs[:, h1 * P:h1 * P + LANES]
            yp = None
            for h, keep_left in ((h1, True), (h2, False)):
                diff = cum[:, h:h + 1] - cum_t[h:h + 1, :]
                w = cbm * jnp.exp(jnp.where(tril, diff, NEG_BIG)) * dt_t[h:h + 1, :]
                xh = jnp.where(left, xp, 0.0) if keep_left else jnp.where(left, 0.0, xp)
                t = jnp.dot(w.astype(BF16), xh.astype(BF16), preferred_element_type=F32)
                yp = t if yp is None else yp + t
            e = jnp.where(left, ecum[:, h1:h1 + 1], ecum[:, h2:h2 + 1])
            y_scr[:, h1 * P:h1 * P + LANES] = yp + e * yoff[:, j * LANES:(j + 1) * LANES]
            we = jnp.where(left, wend[:, h1:h1 + 1], wend[:, h2:h2 + 1])
            xw_parts.append(xp * we)
            cd_parts.append(jnp.where(left1, cdec[:, h1:h1 + 1], cdec[:, h2:h2 + 1]))
        xw = jnp.concatenate(xw_parts, axis=1).astype(BF16)
        cd = jnp.concatenate(cd_parts, axis=1)
        upd = lax.dot_general(bg, xw, (((0,), (0,)), ((), ())), preferred_element_type=F32)
        st_ref[0, g] = cd * st + upd

    y = (y_scr[...] + dskip_ref[...] * xs) * _silu(z_ref[0])
    for g in range(G):
        yg = y[:, g * gw:(g + 1) * gw]
        ms = jnp.mean(yg * yg, axis=-1, keepdims=True)
        y_ref[0, :, g * gw:(g + 1) * gw] = (
            yg * lax.rsqrt(ms + EPS) * ng_ref[:, g * gw:(g + 1) * gw]).astype(y_ref.dtype)


def _ssd_branch(z, xbc, misc, dt_block, ctx, h0_t, cw, cb, dtb, alog, dskip_x, ng, *, H, P, G, N):
    b, T, W = z.shape
    C = xbc.shape[-1]
    KW = cw.shape[0]
    L = min(CHUNK, T)
    hg = H // G
    gw = hg * P
    assert T % L == 0 and L >= KW - 1 and hg % 2 == 0 and 2 * P == LANES and H <= LANES
    kern = functools.partial(_ssd_kernel, L=L, H=H, P=P, G=G, N=N, KW=KW)
    const2 = lambda bi, c: (0, 0)
    return pl.pallas_call(
        kern,
        grid=(b, T // L),
        in_specs=[
            pl.BlockSpec((1, L, W), lambda bi, c: (bi, c, 0)),
            pl.BlockSpec((1, L, C), lambda bi, c: (bi, c, 0)),
            pl.BlockSpec((1, L, LANES), lambda bi, c: (bi, c, dt_block)),
            pl.BlockSpec((1, KW - 1, C), lambda bi, c: (bi, 0, 0)),
            pl.BlockSpec((1, G, N, gw), lambda bi, c: (bi, 0, 0, 0)),
            pl.BlockSpec((KW, C), const2),
            pl.BlockSpec((1, C), const2),
            pl.BlockSpec((1, LANES), const2),
            pl.BlockSpec((1, LANES), const2),
            pl.BlockSpec((1, W), const2),
            pl.BlockSpec((1, W), const2),
        ],
        out_specs=[
            pl.BlockSpec((1, L, W), lambda bi, c: (bi, c, 0)),
            pl.BlockSpec((1, G, N, gw), lambda bi, c: (bi, 0, 0, 0)),
        ],
        out_shape=[jax.ShapeDtypeStruct((b, T, W), BF16), jax.ShapeDtypeStruct((b, G, N, gw), F32)],
        scratch_shapes=[pltpu.VMEM((8 + L, C), F32), pltpu.VMEM((L, W), F32)],
        compiler_params=_params("parallel", "arbitrary"),
        name="ssd_scan",
    )(z, xbc, misc, ctx, h0_t, cw, cb, dtb, alog, dskip_x, ng)


def _swap_halves(x, half):
    lane = lax.broadcasted_iota(jnp.int32, x.shape, 1)
    first = (lane % (2 * half)) < half
    return jnp.where(first, pltpu.roll(x, LANES - half, 1), pltpu.roll(x, half, 1))


def _mla_new_kernel(ckv_ref, kr_ref, cos_ref, sin_ref, g_ref, ckv_out, kr_out, *, R):
    x = ckv_ref[0]
    ms = jnp.mean(x * x, axis=-1, keepdims=True)
    ckv_out[0] = x * lax.rsqrt(ms + EPS) * g_ref[...]
    kr = kr_ref[0]
    rot = kr * cos_ref[...] + _swap_halves(kr, R // 2) * sin_ref[...]
    kr_out[0] = rot[:, :R]


def _mla_new(misc, cos2, sin2, kv_norm_g, *, KVL, R, kr_block):
    b, T, _ = misc.shape
    tb = _pick(T, (256, 128, 64))
    return pl.pallas_call(
        functools.partial(_mla_new_kernel, R=R),
        grid=(b, T // tb),
        in_specs=[
            pl.BlockSpec((1, tb, KVL), lambda bi, t: (bi, t, 0)),
            pl.BlockSpec((1, tb, LANES), lambda bi, t: (bi, t, kr_block)),
            pl.BlockSpec((tb, LANES), lambda bi, t: (t, 0)),
            pl.BlockSpec((tb, LANES), lambda bi, t: (t, 0)),
            pl.BlockSpec((1, KVL), lambda bi, t: (0, 0)),
        ],
        out_specs=[
            pl.BlockSpec((1, tb, KVL), lambda bi, t: (bi, t, 0)),
            pl.BlockSpec((1, tb, R), lambda bi, t: (bi, t, 0)),
        ],
        out_shape=[jax.ShapeDtypeStruct((b, T, KVL), F32), jax.ShapeDtypeStruct((b, T, R), F32)],
        compiler_params=_params("parallel", "parallel"),
        name="mla_new_kv",
    )(misc, misc, cos2, sin2, kv_norm_g.reshape(1, KVL))


def _mla_q_kernel(qn_ref, qr_ref, cos_ref, sin_ref, gn_ref, gr_ref, o_ref, *, HM, NOPE, R, scale):
    lane = lax.broadcasted_iota(jnp.int32, (qn_ref.shape[1], LANES), 1)
    left = lane < R
    for j in range(HM // 2):
        qr = qr_ref[0, :, j * LANES:(j + 1) * LANES]
        rot = qr * cos_ref[...] + _swap_halves(qr, R // 2) * sin_ref[...]
        sq = rot * rot
        for h, keep_left in ((2 * j, True), (2 * j + 1, False)):
            qn = qn_ref[0, :, h * NOPE:(h + 1) * NOPE]
            sel = left if keep_left else jnp.logical_not(left)
            ss = jnp.sum(qn * qn, axis=-1, keepdims=True) + jnp.sum(jnp.where(sel, sq, 0.0), axis=-1, keepdims=True)
            inv = lax.rsqrt(ss * (1.0 / (NOPE + R)) + EPS) * scale
            o_ref[0, :, h * 2 * LANES:h * 2 * LANES + NOPE] = (qn * inv * gn_ref[...]).astype(o_ref.dtype)
            o_ref[0, :, h * 2 * LANES + NOPE:(h + 1) * 2 * LANES] = (
                jnp.where(sel, rot * inv * gr_ref[...], 0.0)).astype(o_ref.dtype)


def _mla_q(pm, cos2, sin2, q_norm_g, *, HM, NOPE, R):
    b, T, _ = pm.shape
    assert NOPE == LANES and 2 * R == LANES and HM % 2 == 0
    tb = _pick(T, (256, 128, 64))
    scale = LOG2E / math.sqrt(NOPE + R)
    gn = q_norm_g[:NOPE].reshape(1, NOPE)
    gr = jnp.tile(q_norm_g[NOPE:], 2).reshape(1, LANES)
    nope_w, rope_w = HM * NOPE, HM * R
    assert nope_w % rope_w == 0
    return pl.pallas_call(
        functools.partial(_mla_q_kernel, HM=HM, NOPE=NOPE, R=R, scale=scale),
        grid=(b, T // tb),
        in_specs=[
            pl.BlockSpec((1, tb, nope_w), lambda bi, t: (bi, t, 0)),
            pl.BlockSpec((1, tb, rope_w), lambda bi, t: (bi, t, nope_w // rope_w)),
            pl.BlockSpec((tb, LANES), lambda bi, t: (t, 0)),
            pl.BlockSpec((tb, LANES), lambda bi, t: (t, 0)),
            pl.BlockSpec((1, NOPE), lambda bi, t: (0, 0)),
            pl.BlockSpec((1, LANES), lambda bi, t: (0, 0)),
        ],
        out_specs=pl.BlockSpec((1, tb, HM * 2 * LANES), lambda bi, t: (bi, t, 0)),
        out_shape=jax.ShapeDtypeStruct((b, T, HM * 2 * LANES), BF16),
        compiler_params=_params("parallel", "parallel"),
        name="mla_q_prep",
    )(pm, pm, cos2, sin2, gn, gr)


def _mla_kv_kernel(ckv_ref, kr_ref, w_ref, gn_ref, gr_ref, k_ref, vt_ref, *, HM, NOPE, R, V):
    ckv = ckv_ref[0, 0].astype(BF16)
    kv = jnp.dot(ckv, w_ref[...], preferred_element_type=F32)
    kr = kr_ref[0, 0]
    ss_r = jnp.sum(kr * kr, axis=-1, keepdims=True)
    krg = kr * gr_ref[...]
    zero = jnp.zeros_like(krg)
    kr_even = jnp.concatenate([krg, zero], axis=1)
    kr_odd = jnp.concatenate([zero, krg], axis=1)
    for h in range(HM):
        base = h * (NOPE + V)
        kn = kv[:, base:base + NOPE]
        ss = jnp.sum(kn * kn, axis=-1, keepdims=True) + ss_r
        inv = lax.rsqrt(ss * (1.0 / (NOPE + R)) + EPS)
        k_ref[0, :, h * 2 * LANES:h * 2 * LANES + NOPE] = (kn * inv * gn_ref[...]).astype(k_ref.dtype)
        k_ref[0, :, h * 2 * LANES + NOPE:(h + 1) * 2 * LANES] = (
            (kr_even if h % 2 == 0 else kr_odd) * inv).astype(k_ref.dtype)
        vt_ref[0, 0, h * V:(h + 1) * V, :] = kv[:, base + NOPE:base + NOPE + V].T.astype(vt_ref.dtype)


def _mla_kv(ckv, krope, layer, ts, w_ukv, k_norm_g, *, HM, NOPE, R, V):
    _, b, S, KVL = ckv.shape
    assert S % ts == 0
    gn = k_norm_g[:NOPE].reshape(1, NOPE)
    gr = k_norm_g[NOPE:].reshape(1, R)
    return pl.pallas_call(
        functools.partial(_mla_kv_kernel, HM=HM, NOPE=NOPE, R=R, V=V),
        grid=(b, S // ts),
        in_specs=[
            pl.BlockSpec((1, 1, ts, KVL), lambda bi, s: (layer, bi, s, 0)),
            pl.BlockSpec((1, 1, ts, R), lambda bi, s: (layer, bi, s, 0)),
            pl.BlockSpec((KVL, HM * (NOPE + V)), lambda bi, s: (0, 0)),
            pl.BlockSpec((1, NOPE), lambda bi, s: (0, 0)),
            pl.BlockSpec((1, R), lambda bi, s: (0, 0)),
        ],
        out_specs=[
            pl.BlockSpec((1, ts, HM * 2 * LANES), lambda bi, s: (bi, s, 0)),
            pl.BlockSpec((1, 1, HM * V, ts), lambda bi, s: (bi, s, 0, 0)),
        ],
        out_shape=[jax.ShapeDtypeStruct((b, S, HM * 2 * LANES), BF16),
                   jax.ShapeDtypeStruct((b, S // ts, HM * V, ts), BF16)],
        compiler_params=_params("parallel", "parallel"),
        name="mla_kv_prep",
    )(ckv, krope, w_ukv, gn, gr)


def _attn_blocks(T, past):
    qb = min(T, 4 * Q_BLOCK)
    kb = min(qb, 2 * Q_BLOCK)
    kbc = _pick(past, (2 * Q_BLOCK, Q_BLOCK, CHUNK)) if past else kb
    assert T % qb == 0 and qb % kb == 0 and kb % CHUNK == 0 and past % CHUNK == 0 and past % kbc == 0
    assert T % Q_BLOCK == 0 or T <= Q_BLOCK
    return qb, kb, kbc


def _mla_scores(q, k):
    return lax.dot_general(k, q, (((1,), (1,)), ((), ())), preferred_element_type=F32)


def _mla_accumulate(ss, vts, carries):
    m_news = [jnp.maximum(m, jnp.max(s, axis=0, keepdims=True)) for s, (m, _, _) in zip(ss, carries)]
    ps = [jnp.exp2(s - mn) for s, mn in zip(ss, m_news)]
    alphas = [jnp.exp2(m - mn) for (m, _, _), mn in zip(carries, m_news)]
    pvs = [jnp.dot(vt, p.astype(BF16), preferred_element_type=F32) for vt, p in zip(vts, ps)]
    return tuple((mn, a * l + jnp.sum(p, axis=0, keepdims=True), a * acc + pv)
                 for mn, a, p, pv, (_, l, acc) in zip(m_news, alphas, ps, pvs, carries))


def _mla_attn_kernel(*refs, QB, KB, KBC, DQ, n_cache, single, shift):
    if n_cache:
        q_ref, kn_ref, vtn_ref, kc_ref, vtc_ref, g_ref, o_ref = refs
    else:
        q_ref, kn_ref, vtn_ref, g_ref, o_ref = refs
    qi = pl.program_id(2)
    nh = q_ref.shape[-1] // DQ
    dv = vtn_ref.shape[2] // nh
    qcols = [slice(hh * DQ, (hh + 1) * DQ) for hh in range(nh)]
    vrows = [slice(hh * dv, (hh + 1) * dv) for hh in range(nh)]
    qs = [q_ref[0, :, c] for c in qcols]
    carries = tuple((jnp.full((1, QB), NEG_BIG, F32), jnp.zeros((1, QB), F32), jnp.zeros((dv, QB), F32))
                    for _ in range(nh))
    nd = QB // KB
    for kk in range(nd):
        r0 = kk * KB
        rows = QB - r0
        start = r0 if single else pl.multiple_of(qi * QB + r0, KB)
        blk = kk if single else qi * nd + kk
        kchunk = lax.shift_right_logical(lax.broadcasted_iota(jnp.int32, (KB, rows), 0), shift)
        rchunk = lax.shift_right_logical(lax.broadcasted_iota(jnp.int32, (KB, rows), 1), shift)
        mask = kchunk <= rchunk
        ss = [jnp.where(mask, _mla_scores(q[r0:], kn_ref[0, pl.ds(start, KB), c]), NEG_BIG) for q, c in zip(qs, qcols)]
        new = _mla_accumulate(ss, [vtn_ref[0, blk, r, :] for r in vrows],
                              [tuple(a[:, r0:] for a in carry) for carry in carries])
        if r0:
            new = tuple(tuple(jnp.concatenate([a[:, :r0], b], axis=1) for a, b in zip(carry, nw))
                        for carry, nw in zip(carries, new))
        carries = new

    def sweep(carries, k_ref, vt_ref, n, per_tile, kb):
        width = per_tile * kb

        def scores(j):
            start = pl.multiple_of(j * width, width)
            return [_mla_scores(q, k_ref[0, pl.ds(start, width), c]) for q, c in zip(qs, qcols)]

        def values(j):
            return [jnp.concatenate([vt_ref[0, j * per_tile + u, r, :] for u in range(per_tile)], axis=1) for r in vrows]

        if not isinstance(n, int):
            return lax.fori_loop(0, n, lambda j, c: _mla_accumulate(scores(j), values(j), c), carries)

        def body(j, state):
            s_next = scores(j + 1)
            return s_next, _mla_accumulate(state[0], values(j), state[1])

        state = lax.fori_loop(0, n - 1, body, (scores(0), carries))
        return _mla_accumulate(state[0], values(n - 1), state[1])

    if not single:
        per_new = 2 if nd % 2 == 0 else 1
        carries = sweep(carries, kn_ref, vtn_ref, (qi * nd) // per_new, per_new, KB)
    if n_cache:
        per_cache = 2 if n_cache % 2 == 0 else 1
        carries = sweep(carries, kc_ref, vtc_ref, n_cache // per_cache, per_cache, KBC)
    for hh, (_, l, acc) in enumerate(carries):
        cols = slice(hh * dv, (hh + 1) * dv)
        o_ref[0, :, cols] = ((acc * (1.0 / l)).T * _silu(g_ref[0, :, cols])).astype(o_ref.dtype)


def _mla_attn(q_full, k_new, vt_new, k_cache, vt_cache, pm, *, HM, V, g_block0):
    b, T, _ = q_full.shape
    assert CHUNK & (CHUNK - 1) == 0 and V == LANES
    past = 0 if k_cache is None else k_cache.shape[1]
    QB, KB, KBC = _attn_blocks(T, past)
    assert vt_new.shape[-1] == KB and (not past or vt_cache.shape[-1] == KBC)
    hp = 2 if (HM % 2 == 0 and g_block0 % 2 == 0) else 1
    dq = 2 * LANES
    kern = functools.partial(_mla_attn_kernel, QB=QB, KB=KB, KBC=KBC, DQ=dq, n_cache=past // KBC, single=T == QB,
                             shift=CHUNK.bit_length() - 1)
    in_specs = [
        pl.BlockSpec((1, QB, hp * dq), lambda bi, h, t: (bi, t, h)),
        pl.BlockSpec((1, T, hp * dq), lambda bi, h, t: (bi, 0, h)),
        pl.BlockSpec((1, T // KB, hp * V, KB), lambda bi, h, t: (bi, 0, h, 0)),
    ]
    args = [q_full, k_new, vt_new]
    if past:
        in_specs += [pl.BlockSpec((1, past, hp * dq), lambda bi, h, t: (bi, 0, h)),
                     pl.BlockSpec((1, past // KBC, hp * V, KBC), lambda bi, h, t: (bi, 0, h, 0))]
        args += [k_cache, vt_cache]
    in_specs.append(pl.BlockSpec((1, QB, hp * V), lambda bi, h, t: (bi, t, g_block0 // hp + h)))
    args.append(pm)
    return pl.pallas_call(
        kern,
        grid=(b, HM // hp, T // QB),
        in_specs=in_specs,
        out_specs=pl.BlockSpec((1, QB, hp * V), lambda bi, h, t: (bi, t, h)),
        out_shape=jax.ShapeDtypeStruct((b, T, HM * V), BF16),
        compiler_params=_params("parallel", "parallel", "arbitrary"),
        name="mla_attention",
    )(*args)


def _strict_lower(n):
    row = lax.broadcasted_iota(jnp.int32, (2 * n, n), 0)
    col = lax.broadcasted_iota(jnp.int32, (2 * n, n), 1)
    return (jnp.where(row >= n, row - n, row) > col).astype(BF16)


def _sb_tiles(qs, ks, vs, carries, tri2, causal):
    zs = [lax.dot_general(q, k, (((1,), (1,)), ((), ())), preferred_element_type=F32) for q, k in zip(qs, ks)]
    mask = None
    if causal:
        mask = lax.broadcasted_iota(jnp.int32, zs[0].shape, 1) < lax.broadcasted_iota(jnp.int32, zs[0].shape, 0)
    sps, logsigs, stacked = [], [], []
    for z in zs:
        neg_abs = lax.bitcast_convert_type(lax.bitcast_convert_type(z, jnp.uint32) | jnp.uint32(0x80000000), F32)
        sp = jnp.log2(1.0 + jnp.exp2(neg_abs)) + jnp.maximum(z, 0.0)
        logsigs.append(z - sp)
        if causal:
            sp = jnp.where(mask, sp, 0.0)
        hi = sp.astype(BF16)
        lo = (sp - hi.astype(F32)).astype(BF16)
        sps.append(sp)
        stacked.append(jnp.concatenate([hi, lo], axis=1))
    laters = [jnp.dot(x, tri2, preferred_element_type=F32) for x in stacked]
    atts = []
    for logsig, later, (run, _) in zip(logsigs, laters, carries):
        att = jnp.exp2(logsig - later - run)
        if causal:
            att = jnp.where(mask, att, 0.0)
        atts.append(att.astype(BF16))
    return tuple((run + jnp.sum(sp, axis=-1, keepdims=True), acc + jnp.dot(att, v, preferred_element_type=F32))
                 for sp, att, v, (run, acc) in zip(sps, atts, vs, carries))


def _sb_tile(q, k, v, run, acc, tri2, causal):
    return _sb_tiles([q], [k], [v], [(run, acc)], tri2, causal)[0]


def _sb_attn_kernel(*refs, QB, KB, KBC, D, n_cache, layer, single, scale):
    if n_cache:
        q_ref, kn_ref, vn_ref, kc_hbm, vc_hbm, g_ref, o_ref, kbuf, vbuf, sem = refs
        bi, head = pl.program_id(0), pl.program_id(1)

        def cache_copies(i):
            start = pl.multiple_of((n_cache - 1 - i) * KBC, KBC)
            slot = i % 2
            return (pltpu.make_async_copy(kc_hbm.at[layer, bi, pl.ds(start, KBC), head, :], kbuf.at[slot], sem.at[0, slot]),
                    pltpu.make_async_copy(vc_hbm.at[layer, bi, pl.ds(start, KBC), head, :], vbuf.at[slot], sem.at[1, slot]))

        for c in cache_copies(0):
            c.start()
    else:
        q_ref, kn_ref, vn_ref, g_ref, o_ref = refs
    qi = pl.program_id(2)
    heads = [slice(hh * D, (hh + 1) * D) for hh in range(q_ref.shape[-1] // D)]
    qs = [(q_ref[0, :, hs] * scale).astype(BF16) for hs in heads]
    tri = _strict_lower(KB)
    carries = tuple((jnp.zeros((QB, 1), F32), jnp.zeros((QB, D), F32)) for _ in heads)
    nd = QB // KB
    for kk in reversed(range(nd)):
        r0 = kk * KB
        start = r0 if single else pl.multiple_of(qi * QB + r0, KB)
        ks = [kn_ref[0, 0, pl.ds(start, KB), hs].astype(BF16) for hs in heads]
        vs = [vn_ref[0, 0, pl.ds(start, KB), hs].astype(BF16) for hs in heads]
        new = _sb_tiles([q[r0:] for q in qs], ks, vs, [(run[r0:], acc[r0:]) for run, acc in carries], tri, True)
        if r0:
            new = tuple((jnp.concatenate([run[:r0], rn], axis=0), jnp.concatenate([acc[:r0], an], axis=0))
                        for (run, acc), (rn, an) in zip(carries, new))
        carries = new

    def sweep(carries, n, load, tri_kb):
        def cond(state):
            i, carries = state
            low = functools.reduce(jnp.minimum, [jnp.min(run) for run, _ in carries])
            return jnp.logical_and(i < n, low < UNDERFLOW_LOG2)

        def body(state):
            i, carries = state
            ks, vs = load(i)
            return i + 1, _sb_tiles(qs, [k.astype(BF16) for k in ks], [v.astype(BF16) for v in vs], carries, tri_kb, False)

        return lax.while_loop(cond, body, (jnp.int32(0), carries))

    n_new = qi * nd

    def load_new(i):
        start = pl.multiple_of((n_new - 1 - i) * KB, KB)
        return ([kn_ref[0, 0, pl.ds(start, KB), hs] for hs in heads], [vn_ref[0, 0, pl.ds(start, KB), hs] for hs in heads])

    _, carries = sweep(carries, n_new, load_new, tri)
    if n_cache:
        def load_cache(i):
            for c in cache_copies(i):
                c.wait()

            @pl.when(i + 1 < n_cache)
            def _():
                for c in cache_copies(i + 1):
                    c.start()

            return [kbuf[i % 2]], [vbuf[i % 2]]

        used, carries = sweep(carries, n_cache, load_cache, tri if KBC == KB else _strict_lower(KBC))

        @pl.when(used < n_cache)
        def _():
            for c in cache_copies(used):
                c.wait()

    for hs, (_, acc) in zip(heads, carries):
        o_ref[0, :, hs] = (acc * _silu(g_ref[0, :, hs])).astype(o_ref.dtype)


def _sb_attn(psb, k_new, v_new, k_cache, v_cache, layer, *, HS, DS):
    b, T, _ = psb.shape
    assert DS == LANES
    past = 0 if k_cache is None else k_cache.shape[2]
    QB, KB, KBC = _attn_blocks(T, past)
    hp = 2 if (HS % 2 == 0 and not past) else 1
    w = hp * DS
    kern = functools.partial(_sb_attn_kernel, QB=QB, KB=KB, KBC=KBC, D=DS, n_cache=past // KBC, layer=layer,
                             single=T == QB, scale=LOG2E / math.sqrt(DS))
    in_specs = [
        pl.BlockSpec((1, QB, w), lambda bi, h, t: (bi, t, h)),
        pl.BlockSpec((1, 1, T, w), lambda bi, h, t: (layer, bi, 0, h)),
        pl.BlockSpec((1, 1, T, w), lambda bi, h, t: (layer, bi, 0, h)),
    ]
    args = [psb, k_new, v_new]
    scratch = []
    if past:
        in_specs += [pl.BlockSpec(memory_space=pl.ANY)] * 2
        args += [k_cache, v_cache]
        scratch = [pltpu.VMEM((2, KBC, DS), k_cache.dtype), pltpu.VMEM((2, KBC, DS), v_cache.dtype),
                   pltpu.SemaphoreType.DMA((2, 2))]
    in_specs.append(pl.BlockSpec((1, QB, w), lambda bi, h, t: (bi, t, HS // hp + h)))
    args.append(psb)
    return pl.pallas_call(
        kern,
        grid=(b, HS // hp, T // QB),
        in_specs=in_specs,
        out_specs=pl.BlockSpec((1, QB, w), lambda bi, h, t: (bi, t, h)),
        out_shape=jax.ShapeDtypeStruct((b, T, HS * DS), BF16),
        scratch_shapes=scratch,
        compiler_params=_params("parallel", "parallel", "arbitrary"),
        name="sb_attention",
    )(*args)


def _merge_kernel(ya_ref, yb_ref, yc_ref, w_ref, g0_ref, g1_ref, g2_ref, bm_ref, o_ref):
    acc = None
    for i, (y_ref, g_ref) in enumerate(((ya_ref, g0_ref), (yb_ref, g1_ref), (yc_ref, g2_ref))):
        gate = _sigmoid(g_ref[...] + bm_ref[i])
        t = gate * jnp.dot(y_ref[...], w_ref[i], preferred_element_type=F32)
        acc = t if acc is None else acc + t
    o_ref[...] = acc.astype(o_ref.dtype)


def _merge(ya, yb, yc, w_branch, gate_logit, b_merge):
    m, k = ya.shape
    nb, _, d = w_branch.shape
    tm = _pick(m, (512, 256, 128, 64, 32, 16))
    tn = _pick(d, (512, 256, 128))
    nt = d // tn
    yspec = pl.BlockSpec((tm, k), lambda i, j: (i, 0))
    gspecs = [pl.BlockSpec((tm, tn), functools.partial(lambda i, j, br: (i, br * nt + j), br=br)) for br in range(nb)]
    return pl.pallas_call(
        _merge_kernel,
        grid=(m // tm, nt),
        in_specs=[yspec, yspec, yspec, pl.BlockSpec((nb, k, tn), lambda i, j: (0, 0, j))] + gspecs
        + [pl.BlockSpec((nb, 1, tn), lambda i, j: (0, 0, j))],
        out_specs=pl.BlockSpec((tm, tn), lambda i, j: (i, j)),
        out_shape=jax.ShapeDtypeStruct((m, d), BF16),
        compiler_params=_params("parallel", "arbitrary"),
        name="branch_merge",
    )(ya, yb, yc, w_branch, gate_logit, gate_logit, gate_logit, b_merge.reshape(nb, 1, d))


def _rope_tables(past, T, R):
    half = R // 2
    freq = ROPE_THETA ** (-jnp.arange(half, dtype=F32) / half)
    ang = (past + jnp.arange(T)).astype(F32)[:, None] * freq[None, :]
    cos, sin = jnp.cos(ang), jnp.sin(ang)
    reps = LANES // R
    return jnp.tile(jnp.concatenate([cos, cos], axis=1), (1, reps)), jnp.tile(jnp.concatenate([-sin, sin], axis=1), (1, reps))


def _prep_layer(dims, norm_g, w_in, conv_w, conv_b, dt_bias, a_log, d_skip, ssd_norm_g, kv_norm_g, w_ukv,
                q_norm_g, k_norm_g, w_branch, b_merge, w_out):
    D = w_in.shape[0]
    W, C, H, P = dims["W"], dims["C"], dims["H"], dims["P"]
    HM, NOPE, R, KVL, HS, DS = dims["HM"], dims["NOPE"], dims["R"], dims["KVL"], dims["HS"], dims["DS"]
    sizes = (W, C, H, HM * (NOPE + R), KVL, R, HM * dims["V"], HS * DS, HS * DS, HS * DS, HS * DS, w_branch.shape[0] * D)
    offs = [0]
    for s in sizes:
        offs.append(offs[-1] + s)
    col = lambda i: w_in[:, offs[i]:offs[i + 1]]
    pad_to = lambda w, n: jnp.pad(w, ((0, 0), (0, n - w.shape[1])))
    wq = col(3).reshape(D, HM, NOPE + R)
    w_mla = jnp.concatenate([wq[:, :, :NOPE].reshape(D, HM * NOPE), wq[:, :, NOPE:].reshape(D, HM * R), col(6)], axis=1)
    w_misc = jnp.concatenate([col(4), pad_to(col(2), LANES), pad_to(col(5), LANES)], axis=1)
    vec = lambda v: jnp.pad(v, (0, LANES - v.shape[0])).reshape(1, LANES)
    return dict(
        norm_g=norm_g,
        w_z=col(0).astype(BF16), w_xbc=col(1).astype(BF16), w_misc=w_misc.astype(BF16), w_mla=w_mla.astype(BF16),
        w_sb=jnp.concatenate([col(7), col(10)], axis=1).astype(BF16),
        w_k=col(8).astype(BF16), w_v=col(9).astype(BF16), w_gate=col(11).astype(BF16),
        conv_w=conv_w, conv_b=conv_b.reshape(1, C), dt_bias=vec(dt_bias), a_log=vec(a_log),
        dskip_x=jnp.repeat(d_skip, P).reshape(1, W), ssd_norm_g=ssd_norm_g.reshape(1, W),
        kv_norm_g=kv_norm_g, w_ukv=w_ukv.astype(BF16), q_norm_g=q_norm_g, k_norm_g=k_norm_g,
        w_branch=w_branch.astype(BF16), b_merge=b_merge, w_out=w_out.astype(BF16),
    )


def _layer(dims, p, h, layer, depth, k_stack, v_stack, sb_k_cache, sb_v_cache, ckv_cache, krope_cache, ssd_h0, conv_ctx):
    b, T, D = h.shape
    M = b * T
    H, P, G, N = dims["H"], dims["P"], dims["G"], dims["N"]
    HM, NOPE, R, V, KVL, HS, DS = dims["HM"], dims["NOPE"], dims["R"], dims["V"], dims["KVL"], dims["HS"], dims["DS"]
    hg = H // G
    past = 0 if sb_k_cache is None else sb_k_cache.shape[2]

    u = _rmsnorm(h.reshape(M, D), p["norm_g"])
    z = _matmul(u, p["w_z"], name="proj_z").reshape(b, T, -1)
    xbc = _matmul(u, p["w_xbc"], name="proj_xbc").reshape(b, T, -1)
    misc = _matmul(u, p["w_misc"], name="proj_misc").reshape(b, T, -1)
    pm = _matmul(u, p["w_mla"], name="proj_mla").reshape(b, T, -1)
    psb = _matmul(u, p["w_sb"], name="proj_sb").reshape(b, T, -1)
    k_stack = _matmul(u, p["w_k"], name="proj_sbk", stack=(k_stack, layer, depth))
    v_stack = _matmul(u, p["w_v"], name="proj_sbv", stack=(v_stack, layer, depth))
    gate_logit = _matmul(u, p["w_gate"], name="proj_gate")

    KW = p["conv_w"].shape[0]
    assert T >= KW - 1
    if ssd_h0 is None:
        h0_t = jnp.zeros((b, G, N, hg * P), F32)
        ctx = jnp.zeros((b, KW - 1, xbc.shape[-1]), F32)
    else:
        h0_t = jnp.swapaxes(ssd_h0.reshape(b, G, hg * P, N), 2, 3)
        ctx = conv_ctx
    y_a, st_t = _ssd_branch(z, xbc, misc, KVL // LANES, ctx, h0_t, p["conv_w"], p["conv_b"], p["dt_bias"],
                            p["a_log"], p["dskip_x"], p["ssd_norm_g"], H=H, P=P, G=G, N=N)
    ssd_new = jnp.swapaxes(st_t, 2, 3).reshape(b, H, P, N)
    conv_new = xbc[:, T - (KW - 1):, :]

    cos2, sin2 = _rope_tables(past, T, R)
    ckv, krope = _mla_new(misc, cos2, sin2, p["kv_norm_g"], KVL=KVL, R=R, kr_block=KVL // LANES + 1)
    q_full = _mla_q(pm, cos2, sin2, p["q_norm_g"], HM=HM, NOPE=NOPE, R=R)
    kv_args = dict(HM=HM, NOPE=NOPE, R=R, V=V)
    _, kb_new, kb_cache = _attn_blocks(T, past)
    k_new, vt_new = _mla_kv(ckv[None], krope[None], 0, kb_new, p["w_ukv"], p["k_norm_g"], **kv_args)
    k_old, vt_old = (_mla_kv(ckv_cache, krope_cache, layer, kb_cache, p["w_ukv"], p["k_norm_g"], **kv_args)
                     if past else (None, None))
    y_b = _mla_attn(q_full, k_new, vt_new, k_old, vt_old, pm, HM=HM, V=V, g_block0=(HM * (NOPE + R)) // V)

    y_c = _sb_attn(psb, k_stack.reshape(depth, b, T, HS * DS), v_stack.reshape(depth, b, T, HS * DS),
                   sb_k_cache, sb_v_cache, layer, HS=HS, DS=DS)

    mixed = _merge(y_a.reshape(M, -1), y_b.reshape(M, -1), y_c.reshape(M, -1), p["w_branch"], gate_logit, p["b_merge"])
    h_new = _matmul(mixed, p["w_out"], residual=h.reshape(M, D), name="out_proj").reshape(b, T, D)
    return h_new, k_stack, v_stack, (ckv, krope, ssd_new, conv_new)


def kernel(x_prompt, x_sample, cache_sb_k, cache_sb_v, cache_mla_ckv, cache_mla_krope, state_ssd, state_conv,
           norm_g, w_in, conv_w, conv_b, dt_bias, a_log, d_skip, ssd_norm_g, mla_kv_norm_g, w_ukv,
           mla_q_norm_g, mla_k_norm_g, w_branch, b_merge, w_out):
    depth = w_in.shape[0]
    H, P, N = state_ssd.shape[2:]
    C = conv_w.shape[-1]
    W = H * P
    HS, DS = cache_sb_k.shape[3:]
    KVL = cache_mla_ckv.shape[-1]
    R = cache_mla_krope.shape[-1]
    NOPE = mla_q_norm_g.shape[-1] - R
    HM = (w_ukv.shape[-1] - w_branch.shape[2]) // NOPE
    V = w_ukv.shape[-1] // HM - NOPE
    dims = dict(H=H, P=P, N=N, C=C, W=W, G=(C - W) // (2 * N), HS=HS, DS=DS, KVL=KVL, R=R, NOPE=NOPE, HM=HM, V=V)
    assert KVL % LANES == 0 and HM * V == HS * DS == W

    hp, hs = x_prompt, x_sample
    kp = vp = ks = vs = None
    outs_p, outs_s = [], []
    for l in range(depth):
        p = _prep_layer(dims, norm_g[l], w_in[l], conv_w[l], conv_b[l], dt_bias[l], a_log[l], d_skip[l], ssd_norm_g[l],
                        mla_kv_norm_g[l], w_ukv[l], mla_q_norm_g[l], mla_k_norm_g[l], w_branch[l], b_merge[l], w_out[l])
        hp, kp, vp, st_p = _layer(dims, p, hp, l, depth, kp, vp, None, None, None, None, None, None)
        outs_p.append(st_p)
        hs, ks, vs, st_s = _layer(dims, p, hs, l, depth, ks, vs, cache_sb_k, cache_sb_v, cache_mla_ckv,
                                  cache_mla_krope, state_ssd[l], state_conv[l])
        outs_s.append(st_s)
    kv_shape = lambda a, h: a.reshape(depth, h.shape[0], h.shape[1], HS, DS)
    stack_p = [jnp.stack(s) for s in zip(*outs_p)]
    stack_s = [jnp.stack(s) for s in zip(*outs_s)]
    return (hp, hs, kv_shape(kp, hp), kv_shape(vp, hp), *stack_p, kv_shape(ks, hs), kv_shape(vs, hs), *stack_s)
```

```python
import functools
import math

import jax
import jax.numpy as jnp
from jax import lax
from jax.experimental import pallas as pl
from jax.experimental.pallas import tpu as pltpu

CHUNK = 64
Q_BLOCK = 128
EPS = 1e-6
ROPE_THETA = 10000.0
LANES = 128
NEG_BIG = -1e30
LOG2E = 1.4426950408889634
UNDERFLOW_LOG2 = 160.0
VMEM_LIMIT_BYTES = 48 * 1024 * 1024

F32 = jnp.float32
BF16 = jnp.bfloat16


def _params(*sem, flags=None):
    return pltpu.CompilerParams(dimension_semantics=sem, vmem_limit_bytes=VMEM_LIMIT_BYTES, flags=flags)


def _pick(n, prefs):
    for p in prefs:
        if n % p == 0:
            return p
    return n


def _sigmoid(x):
    return 1.0 / (1.0 + jnp.exp(-x))


def _silu(x):
    return x * _sigmoid(x)


def _softplus(x):
    return jnp.maximum(x, 0.0) + jnp.log(1.0 + jnp.exp(-jnp.abs(x)))


def _rmsnorm_kernel(x_ref, g_ref, o_ref):
    x = x_ref[...]
    ms = jnp.mean(x * x, axis=-1, keepdims=True)
    o_ref[...] = (x * lax.rsqrt(ms + EPS) * g_ref[...]).astype(o_ref.dtype)


def _rmsnorm(x2d, g):
    m, d = x2d.shape
    tm = _pick(m, (512, 256, 128, 64, 32, 16))
    return pl.pallas_call(
        _rmsnorm_kernel,
        grid=(m // tm,),
        in_specs=[pl.BlockSpec((tm, d), lambda i: (i, 0)), pl.BlockSpec((1, d), lambda i: (0, 0))],
        out_specs=pl.BlockSpec((tm, d), lambda i: (i, 0)),
        out_shape=jax.ShapeDtypeStruct((m, d), BF16),
        compiler_params=_params("parallel"),
        name="rmsnorm",
    )(x2d, g.reshape(1, d))


def _mm_kernel(x_ref, w_ref, o_ref):
    o_ref[...] = jnp.dot(x_ref[...], w_ref[...], preferred_element_type=F32).astype(o_ref.dtype)


def _mm_res_kernel(x_ref, w_ref, r_ref, o_ref):
    o_ref[...] = r_ref[...] + jnp.dot(x_ref[...], w_ref[...], preferred_element_type=F32)


def _mm_into_kernel(x_ref, w_ref, prev_ref, o_ref):
    del prev_ref
    _mm_kernel(x_ref, w_ref, o_ref)


def _mm_first_kernel(x_ref, w_ref, o_ref, *, layer):
    slab = pl.program_id(0)

    @pl.when(slab == layer)
    def _():
        _mm_kernel(x_ref, w_ref, o_ref)

    @pl.when(slab != layer)
    def _():
        o_ref[...] = jnp.zeros_like(o_ref)


def _matmul(x, w, residual=None, name="matmul", stack=None):
    m, k = x.shape
    n = w.shape[1]
    tm = _pick(m, (1024, 512, 256, 128, 64, 32, 16))
    tn = _pick(n, (1024, 512, 256, 128))
    in_specs = [pl.BlockSpec((tm, k), lambda i, j: (i, 0)), pl.BlockSpec((k, tn), lambda i, j: (0, j))]
    args = [x, w]
    kern = _mm_kernel
    out_spec = pl.BlockSpec((tm, tn), lambda i, j: (i, j))
    out_shape = jax.ShapeDtypeStruct((m, n), F32)
    aliases = {}
    if residual is not None:
        in_specs.append(pl.BlockSpec((tm, tn), lambda i, j: (i, j)))
        args.append(residual)
        kern = _mm_res_kernel
    grid = (m // tm, n // tn)
    sem = ("parallel", "arbitrary")
    if stack is not None:
        assert residual is None
        prev, layer, depth = stack
        out_shape = jax.ShapeDtypeStruct((depth, m, n), F32)
        if prev is not None:
            out_spec = pl.BlockSpec((None, tm, tn), lambda i, j: (layer, i, j))
            in_specs.append(pl.BlockSpec(memory_space=pl.ANY))
            args.append(prev)
            kern = _mm_into_kernel
            aliases = {2: 0}
        else:
            here = lambda s, i: jnp.where(s == layer, i, 0)
            in_specs = [pl.BlockSpec((tm, k), lambda s, i, j: (here(s, i), 0)),
                        pl.BlockSpec((k, tn), lambda s, i, j: (0, here(s, j)))]
            out_spec = pl.BlockSpec((None, tm, tn), lambda s, i, j: (s, i, j))
            kern = functools.partial(_mm_first_kernel, layer=layer)
            grid = (depth,) + grid
            sem = ("parallel",) + sem
    return pl.pallas_call(
        kern,
        grid=grid,
        in_specs=in_specs,
        out_specs=out_spec,
        out_shape=out_shape,
        input_output_aliases=aliases,
        compiler_params=_params(*sem),
        name=name,
    )(*args)


def _ssd_kernel(z_ref, xbc_ref, dt_ref, ctx_ref, h0_ref, cw_ref, cb_ref, dtb_ref, alog_ref, dskip_ref, ng_ref,
                y_ref, st_ref, pad_scr, y_scr, *, L, H, P, G, N, KW):
    c = pl.program_id(1)
    W = H * P
    hg = H // G
    gw = hg * P
    top = 8 - (KW - 1)

    @pl.when(c == 0)
    def _():
        pad_scr[top:8, :] = ctx_ref[0]
        st_ref[0] = h0_ref[0]

    pad_scr[8:8 + L, :] = xbc_ref[0]
    conv = cb_ref[...] + cw_ref[0:1, :] * pad_scr[top:top + L, :]
    for j in range(1, KW):
        conv = conv + cw_ref[j:j + 1, :] * pad_scr[top + j:top + j + L, :]
    pad_scr[top:8, :] = pad_scr[8 + L - (KW - 1):8 + L, :]
    conv = _silu(conv)
    xs = conv[:, :W]
    bm = conv[:, W:W + G * N]
    cm = conv[:, W + G * N:]

    dt = _softplus(dt_ref[0] + dtb_ref[...])
    a = -jnp.exp(alog_ref[...])
    da = dt * a
    row = lax.broadcasted_iota(jnp.int32, (L, L), 0)
    col = lax.broadcasted_iota(jnp.int32, (L, L), 1)
    tril = col <= row
    cum = jnp.dot(tril.astype(F32), da, precision=lax.Precision.HIGHEST, preferred_element_type=F32)
    cum_t = cum.T
    dt_t = dt.T
    cum_last = cum[L - 1:L, :]
    wend = jnp.exp(cum_last - cum) * dt
    ecum = jnp.exp(cum)
    cdec = jnp.exp(cum_last)
    left = lax.broadcasted_iota(jnp.int32, (L, LANES), 1) < P
    left1 = lax.broadcasted_iota(jnp.int32, (1, LANES), 1) < P

    for g in range(G):
        bg = bm[:, g * N:(g + 1) * N].astype(BF16)
        cg = cm[:, g * N:(g + 1) * N].astype(BF16)
        cbm = lax.dot_general(cg, bg, (((1,), (1,)), ((), ())), preferred_element_type=F32)
        st = st_ref[0, g]
        yoff = jnp.dot(cg, st.astype(BF16), preferred_element_type=F32)
        xw_parts, cd_parts = [], []
        for j in range(hg // 2):
            h1 = g * hg + 2 * j
            h2 = h1 + 1
            xp = xs[:, h1 * P:h1 * P + LANES]
            yp = None
            for h, keep_left in ((h1, True), (h2, False)):
                diff = cum[:, h:h + 1] - cum_t[h:h + 1, :]
                w = cbm * jnp.exp(jnp.where(tril, diff, NEG_BIG)) * dt_t[h:h + 1, :]
                xh = jnp.where(left, xp, 0.0) if keep_left else jnp.where(left, 0.0, xp)
                t = jnp.dot(w.astype(BF16), xh.astype(BF16), preferred_element_type=F32)
                yp = t if yp is None else yp + t
            e = jnp.where(left, ecum[:, h1:h1 + 1], ecum[:, h2:h2 + 1])
            y_scr[:, h1 * P:h1 * P + LANES] = yp + e * yoff[:, j * LANES:(j + 1) * LANES]
            we = jnp.where(left, wend[:, h1:h1 + 1], wend[:, h2:h2 + 1])
            xw_parts.append(xp * we)
            cd_parts.append(jnp.where(left1, cdec[:, h1:h1 + 1], cdec[:, h2:h2 + 1]))
        xw = jnp.concatenate(xw_parts, axis=1).astype(BF16)
        cd = jnp.concatenate(cd_parts, axis=1)
        upd = lax.dot_general(bg, xw, (((0,), (0,)), ((), ())), preferred_element_type=F32)
        st_ref[0, g] = cd * st + upd

    y = (y_scr[...] + dskip_ref[...] * xs) * _silu(z_ref[0])
    for g in range(G):
        yg = y[:, g * gw:(g + 1) * gw]
        ms = jnp.mean(yg * yg, axis=-1, keepdims=True)
        y_ref[0, :, g * gw:(g + 1) * gw] = (
            yg * lax.rsqrt(ms + EPS) * ng_ref[:, g * gw:(g + 1) * gw]).astype(y_ref.dtype)


def _ssd_branch(z, xbc, misc, dt_block, ctx, h0_t, cw, cb, dtb, alog, dskip_x, ng, *, H, P, G, N):
    b, T, W = z.shape
    C = xbc.shape[-1]
    KW = cw.shape[0]
    L = min(CHUNK, T)
    hg = H // G
    gw = hg * P
    assert T % L == 0 and L >= KW - 1 and hg % 2 == 0 and 2 * P == LANES and H <= LANES
    kern = functools.partial(_ssd_kernel, L=L, H=H, P=P, G=G, N=N, KW=KW)
    const2 = lambda bi, c: (0, 0)
    return pl.pallas_call(
        kern,
        grid=(b, T // L),
        in_specs=[
            pl.BlockSpec((1, L, W), lambda bi, c: (bi, c, 0)),
            pl.BlockSpec((1, L, C), lambda bi, c: (bi, c, 0)),
            pl.BlockSpec((1, L, LANES), lambda bi, c: (bi, c, dt_block)),
            pl.BlockSpec((1, KW - 1, C), lambda bi, c: (bi, 0, 0)),
            pl.BlockSpec((1, G, N, gw), lambda bi, c: (bi, 0, 0, 0)),
            pl.BlockSpec((KW, C), const2),
            pl.BlockSpec((1, C), const2),
            pl.BlockSpec((1, LANES), const2),
            pl.BlockSpec((1, LANES), const2),
            pl.BlockSpec((1, W), const2),
            pl.BlockSpec((1, W), const2),
        ],
        out_specs=[
            pl.BlockSpec((1, L, W), lambda bi, c: (bi, c, 0)),
            pl.BlockSpec((1, G, N, gw), lambda bi, c: (bi, 0, 0, 0)),
        ],
        out_shape=[jax.ShapeDtypeStruct((b, T, W), BF16), jax.ShapeDtypeStruct((b, G, N, gw), F32)],
        scratch_shapes=[pltpu.VMEM((8 + L, C), F32), pltpu.VMEM((L, W), F32)],
        compiler_params=_params("parallel", "arbitrary"),
        name="ssd_scan",
    )(z, xbc, misc, ctx, h0_t, cw, cb, dtb, alog, dskip_x, ng)


def _swap_halves(x, half):
    lane = lax.broadcasted_iota(jnp.int32, x.shape, 1)
    first = (lane % (2 * half)) < half
    return jnp.where(first, pltpu.roll(x, LANES - half, 1), pltpu.roll(x, half, 1))


def _mla_new_kernel(ckv_ref, kr_ref, cos_ref, sin_ref, g_ref, ckv_out, kr_out, *, R):
    x = ckv_ref[0]
    ms = jnp.mean(x * x, axis=-1, keepdims=True)
    ckv_out[0] = x * lax.rsqrt(ms + EPS) * g_ref[...]
    kr = kr_ref[0]
    rot = kr * cos_ref[...] + _swap_halves(kr, R // 2) * sin_ref[...]
    kr_out[0] = rot[:, :R]


def _mla_new(misc, cos2, sin2, kv_norm_g, *, KVL, R, kr_block):
    b, T, _ = misc.shape
    tb = _pick(T, (256, 128, 64))
    return pl.pallas_call(
        functools.partial(_mla_new_kernel, R=R),
        grid=(b, T // tb),
        in_specs=[
            pl.BlockSpec((1, tb, KVL), lambda bi, t: (bi, t, 0)),
            pl.BlockSpec((1, tb, LANES), lambda bi, t: (bi, t, kr_block)),
            pl.BlockSpec((tb, LANES), lambda bi, t: (t, 0)),
            pl.BlockSpec((tb, LANES), lambda bi, t: (t, 0)),
            pl.BlockSpec((1, KVL), lambda bi, t: (0, 0)),
        ],
        out_specs=[
            pl.BlockSpec((1, tb, KVL), lambda bi, t: (bi, t, 0)),
            pl.BlockSpec((1, tb, R), lambda bi, t: (bi, t, 0)),
        ],
        out_shape=[jax.ShapeDtypeStruct((b, T, KVL), F32), jax.ShapeDtypeStruct((b, T, R), F32)],
        compiler_params=_params("parallel", "parallel"),
        name="mla_new_kv",
    )(misc, misc, cos2, sin2, kv_norm_g.reshape(1, KVL))


def _mla_q_kernel(qn_ref, qr_ref, cos_ref, sin_ref, gn_ref, gr_ref, o_ref, *, HM, NOPE, R, scale):
    lane = lax.broadcasted_iota(jnp.int32, (qn_ref.shape[1], LANES), 1)
    left = lane < R
    for j in range(HM // 2):
        qr = qr_ref[0, :, j * LANES:(j + 1) * LANES]
        rot = qr * cos_ref[...] + _swap_halves(qr, R // 2) * sin_ref[...]
        sq = rot * rot
        for h, keep_left in ((2 * j, True), (2 * j + 1, False)):
            qn = qn_ref[0, :, h * NOPE:(h + 1) * NOPE]
            sel = left if keep_left else jnp.logical_not(left)
            ss = jnp.sum(qn * qn, axis=-1, keepdims=True) + jnp.sum(jnp.where(sel, sq, 0.0), axis=-1, keepdims=True)
            inv = lax.rsqrt(ss * (1.0 / (NOPE + R)) + EPS) * scale
            o_ref[0, :, h * 2 * LANES:h * 2 * LANES + NOPE] = (qn * inv * gn_ref[...]).astype(o_ref.dtype)
            o_ref[0, :, h * 2 * LANES + NOPE:(h + 1) * 2 * LANES] = (
                jnp.where(sel, rot * inv * gr_ref[...], 0.0)).astype(o_ref.dtype)


def _mla_q(pm, cos2, sin2, q_norm_g, *, HM, NOPE, R):
    b, T, _ = pm.shape
    assert NOPE == LANES and 2 * R == LANES and HM % 2 == 0
    tb = _pick(T, (256, 128, 64))
    scale = LOG2E / math.sqrt(NOPE + R)
    gn = q_norm_g[:NOPE].reshape(1, NOPE)
    gr = jnp.tile(q_norm_g[NOPE:], 2).reshape(1, LANES)
    nope_w, rope_w = HM * NOPE, HM * R
    assert nope_w % rope_w == 0
    return pl.pallas_call(
        functools.partial(_mla_q_kernel, HM=HM, NOPE=NOPE, R=R, scale=scale),
        grid=(b, T // tb),
        in_specs=[
            pl.BlockSpec((1, tb, nope_w), lambda bi, t: (bi, t, 0)),
            pl.BlockSpec((1, tb, rope_w), lambda bi, t: (bi, t, nope_w // rope_w)),
            pl.BlockSpec((tb, LANES), lambda bi, t: (t, 0)),
            pl.BlockSpec((tb, LANES), lambda bi, t: (t, 0)),
            pl.BlockSpec((1, NOPE), lambda bi, t: (0, 0)),
            pl.BlockSpec((1, LANES), lambda bi, t: (0, 0)),
        ],
        out_specs=pl.BlockSpec((1, tb, HM * 2 * LANES), lambda bi, t: (bi, t, 0)),
        out_shape=jax.ShapeDtypeStruct((b, T, HM * 2 * LANES), BF16),
        compiler_params=_params("parallel", "parallel"),
        name="mla_q_prep",
    )(pm, pm, cos2, sin2, gn, gr)


def _mla_kv_kernel(ckv_ref, kr_ref, w_ref, gn_ref, gr_ref, k_ref, vt_ref, *, HM, NOPE, R, V):
    ckv = ckv_ref[0, 0].astype(BF16)
    kv = jnp.dot(ckv, w_ref[...], preferred_element_type=F32)
    kr = kr_ref[0, 0]
    ss_r = jnp.sum(kr * kr, axis=-1, keepdims=True)
    krg = kr * gr_ref[...]
    zero = jnp.zeros_like(krg)
    kr_even = jnp.concatenate([krg, zero], axis=1)
    kr_odd = jnp.concatenate([zero, krg], axis=1)
    for h in range(HM):
        base = h * (NOPE + V)
        kn = kv[:, base:base + NOPE]
        ss = jnp.sum(kn * kn, axis=-1, keepdims=True) + ss_r
        inv = lax.rsqrt(ss * (1.0 / (NOPE + R)) + EPS)
        k_ref[0, :, h * 2 * LANES:h * 2 * LANES + NOPE] = (kn * inv * gn_ref[...]).astype(k_ref.dtype)
        k_ref[0, :, h * 2 * LANES + NOPE:(h + 1) * 2 * LANES] = (
            (kr_even if h % 2 == 0 else kr_odd) * inv).astype(k_ref.dtype)
        vt_ref[0, 0, h * V:(h + 1) * V, :] = kv[:, base + NOPE:base + NOPE + V].T.astype(vt_ref.dtype)


def _mla_kv(ckv, krope, layer, ts, w_ukv, k_norm_g, *, HM, NOPE, R, V):
    _, b, S, KVL = ckv.shape
    assert S % ts == 0
    gn = k_norm_g[:NOPE].reshape(1, NOPE)
    gr = k_norm_g[NOPE:].reshape(1, R)
    return pl.pallas_call(
        functools.partial(_mla_kv_kernel, HM=HM, NOPE=NOPE, R=R, V=V),
        grid=(b, S // ts),
        in_specs=[
            pl.BlockSpec((1, 1, ts, KVL), lambda bi, s: (layer, bi, s, 0)),
            pl.BlockSpec((1, 1, ts, R), lambda bi, s: (layer, bi, s, 0)),
            pl.BlockSpec((KVL, HM * (NOPE + V)), lambda bi, s: (0, 0)),
            pl.BlockSpec((1, NOPE), lambda bi, s: (0, 0)),
            pl.BlockSpec((1, R), lambda bi, s: (0, 0)),
        ],
        out_specs=[
            pl.BlockSpec((1, ts, HM * 2 * LANES), lambda bi, s: (bi, s, 0)),
            pl.BlockSpec((1, 1, HM * V, ts), lambda bi, s: (bi, s, 0, 0)),
        ],
        out_shape=[jax.ShapeDtypeStruct((b, S, HM * 2 * LANES), BF16),
                   jax.ShapeDtypeStruct((b, S // ts, HM * V, ts), BF16)],
        compiler_params=_params("parallel", "parallel"),
        name="mla_kv_prep",
    )(ckv, krope, w_ukv, gn, gr)


def _attn_blocks(T, past):
    qb = min(T, 4 * Q_BLOCK)
    kb = min(qb, 2 * Q_BLOCK)
    kbc = _pick(past, (2 * Q_BLOCK, Q_BLOCK, CHUNK)) if past else kb
    assert T % qb == 0 and qb % kb == 0 and kb % CHUNK == 0 and past % CHUNK == 0 and past % kbc == 0
    assert T % Q_BLOCK == 0 or T <= Q_BLOCK
    return qb, kb, kbc


def _mla_scores(q, k):
    return lax.dot_general(k, q, (((1,), (1,)), ((), ())), preferred_element_type=F32)


def _mla_accumulate(ss, vts, carries):
    m_news = [jnp.maximum(m, jnp.max(s, axis=0, keepdims=True)) for s, (m, _, _) in zip(ss, carries)]
    ps = [jnp.exp2(s - mn) for s, mn in zip(ss, m_news)]
    alphas = [jnp.exp2(m - mn) for (m, _, _), mn in zip(carries, m_news)]
    pvs = [jnp.dot(vt, p.astype(BF16), preferred_element_type=F32) for vt, p in zip(vts, ps)]
    return tuple((mn, a * l + jnp.sum(p, axis=0, keepdims=True), a * acc + pv)
                 for mn, a, p, pv, (_, l, acc) in zip(m_news, alphas, ps, pvs, carries))


def _mla_attn_kernel(*refs, QB, KB, KBC, DQ, n_cache, single, shift):
    if n_cache:
        q_ref, kn_ref, vtn_ref, kc_ref, vtc_ref, g_ref, o_ref = refs
    else:
        q_ref, kn_ref, vtn_ref, g_ref, o_ref = refs
    qi = pl.program_id(2)
    nh = q_ref.shape[-1] // DQ
    dv = vtn_ref.shape[2] // nh
    qcols = [slice(hh * DQ, (hh + 1) * DQ) for hh in range(nh)]
    vrows = [slice(hh * dv, (hh + 1) * dv) for hh in range(nh)]
    qs = [q_ref[0, :, c] for c in qcols]
    carries = tuple((jnp.full((1, QB), NEG_BIG, F32), jnp.zeros((1, QB), F32), jnp.zeros((dv, QB), F32))
                    for _ in range(nh))
    nd = QB // KB
    for kk in range(nd):
        r0 = kk * KB
        rows = QB - r0
        start = r0 if single else pl.multiple_of(qi * QB + r0, KB)
        blk = kk if single else qi * nd + kk
        kchunk = lax.shift_right_logical(lax.broadcasted_iota(jnp.int32, (KB, rows), 0), shift)
        rchunk = lax.shift_right_logical(lax.broadcasted_iota(jnp.int32, (KB, rows), 1), shift)
        mask = kchunk <= rchunk
        ss = [jnp.where(mask, _mla_scores(q[r0:], kn_ref[0, pl.ds(start, KB), c]), NEG_BIG) for q, c in zip(qs, qcols)]
        new = _mla_accumulate(ss, [vtn_ref[0, blk, r, :] for r in vrows],
                              [tuple(a[:, r0:] for a in carry) for carry in carries])
        if r0:
            new = tuple(tuple(jnp.concatenate([a[:, :r0], b], axis=1) for a, b in zip(carry, nw))
                        for carry, nw in zip(carries, new))
        carries = new

    def sweep(carries, k_ref, vt_ref, n, per_tile, kb):
        width = per_tile * kb

        def scores(j):
            start = pl.multiple_of(j * width, width)
            return [_mla_scores(q, k_ref[0, pl.ds(start, width), c]) for q, c in zip(qs, qcols)]

        def values(j):
            return [jnp.concatenate([vt_ref[0, j * per_tile + u, r, :] for u in range(per_tile)], axis=1) for r in vrows]

        if not isinstance(n, int):
            return lax.fori_loop(0, n, lambda j, c: _mla_accumulate(scores(j), values(j), c), carries)

        def body(j, state):
            s_next = scores(j + 1)
            return s_next, _mla_accumulate(state[0], values(j), state[1])

        state = lax.fori_loop(0, n - 1, body, (scores(0), carries))
        return _mla_accumulate(state[0], values(n - 1), state[1])

    if not single:
        per_new = 2 if nd % 2 == 0 else 1
        carries = sweep(carries, kn_ref, vtn_ref, (qi * nd) // per_new, per_new, KB)
    if n_cache:
        per_cache = 2 if n_cache % 2 == 0 else 1
        carries = sweep(carries, kc_ref, vtc_ref, n_cache // per_cache, per_cache, KBC)
    for hh, (_, l, acc) in enumerate(carries):
        cols = slice(hh * dv, (hh + 1) * dv)
        o_ref[0, :, cols] = ((acc * (1.0 / l)).T * _silu(g_ref[0, :, cols])).astype(o_ref.dtype)


def _mla_attn(q_full, k_new, vt_new, k_cache, vt_cache, pm, *, HM, V, g_block0):
    b, T, _ = q_full.shape
    assert CHUNK & (CHUNK - 1) == 0 and V == LANES
    past = 0 if k_cache is None else k_cache.shape[1]
    QB, KB, KBC = _attn_blocks(T, past)
    assert vt_new.shape[-1] == KB and (not past or vt_cache.shape[-1] == KBC)
    hp = next(n for n in (4, 2, 1) if HM % n == 0 and g_block0 % n == 0)
    dq = 2 * LANES
    kern = functools.partial(_mla_attn_kernel, QB=QB, KB=KB, KBC=KBC, DQ=dq, n_cache=past // KBC, single=T == QB,
                             shift=CHUNK.bit_length() - 1)
    in_specs = [
        pl.BlockSpec((1, QB, hp * dq), lambda bi, h, t: (bi, t, h)),
        pl.BlockSpec((1, T, hp * dq), lambda bi, h, t: (bi, 0, h)),
        pl.BlockSpec((1, T // KB, hp * V, KB), lambda bi, h, t: (bi, 0, h, 0)),
    ]
    args = [q_full, k_new, vt_new]
    if past:
        in_specs += [pl.BlockSpec((1, past, hp * dq), lambda bi, h, t: (bi, 0, h)),
                     pl.BlockSpec((1, past // KBC, hp * V, KBC), lambda bi, h, t: (bi, 0, h, 0))]
        args += [k_cache, vt_cache]
    in_specs.append(pl.BlockSpec((1, QB, hp * V), lambda bi, h, t: (bi, t, g_block0 // hp + h)))
    args.append(pm)
    return pl.pallas_call(
        kern,
        grid=(b, HM // hp, T // QB),
        in_specs=in_specs,
        out_specs=pl.BlockSpec((1, QB, hp * V), lambda bi, h, t: (bi, t, h)),
        out_shape=jax.ShapeDtypeStruct((b, T, HM * V), BF16),
        compiler_params=_params("parallel", "parallel", "arbitrary"),
        name="mla_attention",
    )(*args)


def _strict_lower(n):
    row = lax.broadcasted_iota(jnp.int32, (2 * n, n), 0)
    col = lax.broadcasted_iota(jnp.int32, (2 * n, n), 1)
    return (jnp.where(row >= n, row - n, row) > col).astype(BF16)


def _sb_tiles(qs, ks, vs, carries, tri2, causal):
    zs = [lax.dot_general(q, k, (((1,), (1,)), ((), ())), preferred_element_type=F32) for q, k in zip(qs, ks)]
    mask = None
    if causal:
        mask = lax.broadcasted_iota(jnp.int32, zs[0].shape, 1) < lax.broadcasted_iota(jnp.int32, zs[0].shape, 0)
    sps, logsigs, stacked = [], [], []
    for z in zs:
        neg_abs = lax.bitcast_convert_type(lax.bitcast_convert_type(z, jnp.uint32) | jnp.uint32(0x80000000), F32)
        sp = jnp.log2(1.0 + jnp.exp2(neg_abs)) + jnp.maximum(z, 0.0)
        logsigs.append(z - sp)
        if causal:
            sp = jnp.where(mask, sp, 0.0)
        hi = sp.astype(BF16)
        lo = (sp - hi.astype(F32)).astype(BF16)
        sps.append(sp)
        stacked.append(jnp.concatenate([hi, lo], axis=1))
    laters = [jnp.dot(x, tri2, preferred_element_type=F32) for x in stacked]
    atts = []
    for logsig, later, (run, _) in zip(logsigs, laters, carries):
        att = jnp.exp2(logsig - later - run)
        if causal:
            att = jnp.where(mask, att, 0.0)
        atts.append(att.astype(BF16))
    return tuple((run + jnp.sum(sp, axis=-1, keepdims=True), acc + jnp.dot(att, v, preferred_element_type=F32))
                 for sp, att, v, (run, acc) in zip(sps, atts, vs, carries))


def _sb_tile(q, k, v, run, acc, tri2, causal):
    return _sb_tiles([q], [k], [v], [(run, acc)], tri2, causal)[0]


def _sb_attn_kernel(*refs, QB, KB, KBC, D, n_cache, layer, single, scale):
    if n_cache:
        q_ref, kn_ref, vn_ref, kc_hbm, vc_hbm, g_ref, o_ref, kbuf, vbuf, sem = refs
        bi, head = pl.program_id(0), pl.program_id(1)

        def cache_copies(i):
            start = pl.multiple_of((n_cache - 1 - i) * KBC, KBC)
            slot = i % 2
            return (pltpu.make_async_copy(kc_hbm.at[layer, bi, pl.ds(start, KBC), head, :], kbuf.at[slot], sem.at[0, slot]),
                    pltpu.make_async_copy(vc_hbm.at[layer, bi, pl.ds(start, KBC), head, :], vbuf.at[slot], sem.at[1, slot]))

        for c in cache_copies(0):
            c.start()
    else:
        q_ref, kn_ref, vn_ref, g_ref, o_ref = refs
    qi = pl.program_id(2)
    heads = [slice(hh * D, (hh + 1) * D) for hh in range(q_ref.shape[-1] // D)]
    qs = [(q_ref[0, :, hs] * scale).astype(BF16) for hs in heads]
    tri = _strict_lower(KB)
    carries = tuple((jnp.zeros((QB, 1), F32), jnp.zeros((QB, D), F32)) for _ in heads)
    nd = QB // KB
    for kk in reversed(range(nd)):
        r0 = kk * KB
        start = r0 if single else pl.multiple_of(qi * QB + r0, KB)
        ks = [kn_ref[0, 0, pl.ds(start, KB), hs].astype(BF16) for hs in heads]
        vs = [vn_ref[0, 0, pl.ds(start, KB), hs].astype(BF16) for hs in heads]
        new = _sb_tiles([q[r0:] for q in qs], ks, vs, [(run[r0:], acc[r0:]) for run, acc in carries], tri, True)
        if r0:
            new = tuple((jnp.concatenate([run[:r0], rn], axis=0), jnp.concatenate([acc[:r0], an], axis=0))
                        for (run, acc), (rn, an) in zip(carries, new))
        carries = new

    def sweep(carries, n, load, tri_kb):
        def cond(state):
            i, carries = state
            low = functools.reduce(jnp.minimum, [jnp.min(run) for run, _ in carries])
            return jnp.logical_and(i < n, low < UNDERFLOW_LOG2)

        def body(state):
            i, carries = state
            ks, vs = load(i)
            return i + 1, _sb_tiles(qs, [k.astype(BF16) for k in ks], [v.astype(BF16) for v in vs], carries, tri_kb, False)

        return lax.while_loop(cond, body, (jnp.int32(0), carries))

    n_new = qi * nd

    def load_new(i):
        start = pl.multiple_of((n_new - 1 - i) * KB, KB)
        return ([kn_ref[0, 0, pl.ds(start, KB), hs] for hs in heads], [vn_ref[0, 0, pl.ds(start, KB), hs] for hs in heads])

    _, carries = sweep(carries, n_new, load_new, tri)
    if n_cache:
        def load_cache(i):
            for c in cache_copies(i):
                c.wait()

            @pl.when(i + 1 < n_cache)
            def _():
                for c in cache_copies(i + 1):
                    c.start()

            return [kbuf[i % 2]], [vbuf[i % 2]]

        used, carries = sweep(carries, n_cache, load_cache, tri if KBC == KB else _strict_lower(KBC))

        @pl.when(used < n_cache)
        def _():
            for c in cache_copies(used):
                c.wait()

    for hs, (_, acc) in zip(heads, carries):
        o_ref[0, :, hs] = (acc * _silu(g_ref[0, :, hs])).astype(o_ref.dtype)


def _sb_attn(psb, k_new, v_new, k_cache, v_cache, layer, *, HS, DS):
    b, T, _ = psb.shape
    assert DS == LANES
    past = 0 if k_cache is None else k_cache.shape[2]
    QB, KB, KBC = _attn_blocks(T, past)
    hp = 2 if (HS % 2 == 0 and not past) else 1
    w = hp * DS
    kern = functools.partial(_sb_attn_kernel, QB=QB, KB=KB, KBC=KBC, D=DS, n_cache=past // KBC, layer=layer,
                             single=T == QB, scale=LOG2E / math.sqrt(DS))
    in_specs = [
        pl.BlockSpec((1, QB, w), lambda bi, h, t: (bi, t, h)),
        pl.BlockSpec((1, 1, T, w), lambda bi, h, t: (layer, bi, 0, h)),
        pl.BlockSpec((1, 1, T, w), lambda bi, h, t: (layer, bi, 0, h)),
    ]
    args = [psb, k_new, v_new]
    scratch = []
    if past:
        in_specs += [pl.BlockSpec(memory_space=pl.ANY)] * 2
        args += [k_cache, v_cache]
        scratch = [pltpu.VMEM((2, KBC, DS), k_cache.dtype), pltpu.VMEM((2, KBC, DS), v_cache.dtype),
                   pltpu.SemaphoreType.DMA((2, 2))]
    in_specs.append(pl.BlockSpec((1, QB, w), lambda bi, h, t: (bi, t, HS // hp + h)))
    args.append(psb)
    return pl.pallas_call(
        kern,
        grid=(b, HS // hp, T // QB),
        in_specs=in_specs,
        out_specs=pl.BlockSpec((1, QB, w), lambda bi, h, t: (bi, t, h)),
        out_shape=jax.ShapeDtypeStruct((b, T, HS * DS), BF16),
        scratch_shapes=scratch,
        compiler_params=_params("parallel", "parallel", "arbitrary"),
        name="sb_attention",
    )(*args)


def _merge_kernel(ya_ref, yb_ref, yc_ref, w_ref, g0_ref, g1_ref, g2_ref, bm_ref, o_ref):
    acc = None
    for i, (y_ref, g_ref) in enumerate(((ya_ref, g0_ref), (yb_ref, g1_ref), (yc_ref, g2_ref))):
        gate = _sigmoid(g_ref[...] + bm_ref[i])
        t = gate * jnp.dot(y_ref[...], w_ref[i], preferred_element_type=F32)
        acc = t if acc is None else acc + t
    o_ref[...] = acc.astype(o_ref.dtype)


def _merge(ya, yb, yc, w_branch, gate_logit, b_merge):
    m, k = ya.shape
    nb, _, d = w_branch.shape
    tm = _pick(m, (512, 256, 128, 64, 32, 16))
    tn = _pick(d, (512, 256, 128))
    nt = d // tn
    yspec = pl.BlockSpec((tm, k), lambda i, j: (i, 0))
    gspecs = [pl.BlockSpec((tm, tn), functools.partial(lambda i, j, br: (i, br * nt + j), br=br)) for br in range(nb)]
    return pl.pallas_call(
        _merge_kernel,
        grid=(m // tm, nt),
        in_specs=[yspec, yspec, yspec, pl.BlockSpec((nb, k, tn), lambda i, j: (0, 0, j))] + gspecs
        + [pl.BlockSpec((nb, 1, tn), lambda i, j: (0, 0, j))],
        out_specs=pl.BlockSpec((tm, tn), lambda i, j: (i, j)),
        out_shape=jax.ShapeDtypeStruct((m, d), BF16),
        compiler_params=_params("parallel", "arbitrary"),
        name="branch_merge",
    )(ya, yb, yc, w_branch, gate_logit, gate_logit, gate_logit, b_merge.reshape(nb, 1, d))


def _rope_tables(past, T, R):
    half = R // 2
    freq = ROPE_THETA ** (-jnp.arange(half, dtype=F32) / half)
    ang = (past + jnp.arange(T)).astype(F32)[:, None] * freq[None, :]
    cos, sin = jnp.cos(ang), jnp.sin(ang)
    reps = LANES // R
    return jnp.tile(jnp.concatenate([cos, cos], axis=1), (1, reps)), jnp.tile(jnp.concatenate([-sin, sin], axis=1), (1, reps))


def _prep_layer(dims, norm_g, w_in, conv_w, conv_b, dt_bias, a_log, d_skip, ssd_norm_g, kv_norm_g, w_ukv,
                q_norm_g, k_norm_g, w_branch, b_merge, w_out):
    D = w_in.shape[0]
    W, C, H, P = dims["W"], dims["C"], dims["H"], dims["P"]
    HM, NOPE, R, KVL, HS, DS = dims["HM"], dims["NOPE"], dims["R"], dims["KVL"], dims["HS"], dims["DS"]
    sizes = (W, C, H, HM * (NOPE + R), KVL, R, HM * dims["V"], HS * DS, HS * DS, HS * DS, HS * DS, w_branch.shape[0] * D)
    offs = [0]
    for s in sizes:
        offs.append(offs[-1] + s)
    col = lambda i: w_in[:, offs[i]:offs[i + 1]]
    pad_to = lambda w, n: jnp.pad(w, ((0, 0), (0, n - w.shape[1])))
    wq = col(3).reshape(D, HM, NOPE + R)
    w_mla = jnp.concatenate([wq[:, :, :NOPE].reshape(D, HM * NOPE), wq[:, :, NOPE:].reshape(D, HM * R), col(6)], axis=1)
    w_misc = jnp.concatenate([col(4), pad_to(col(2), LANES), pad_to(col(5), LANES)], axis=1)
    vec = lambda v: jnp.pad(v, (0, LANES - v.shape[0])).reshape(1, LANES)
    return dict(
        norm_g=norm_g,
        w_z=col(0).astype(BF16), w_xbc=col(1).astype(BF16), w_misc=w_misc.astype(BF16), w_mla=w_mla.astype(BF16),
        w_sb=jnp.concatenate([col(7), col(10)], axis=1).astype(BF16),
        w_k=col(8).astype(BF16), w_v=col(9).astype(BF16), w_gate=col(11).astype(BF16),
        conv_w=conv_w, conv_b=conv_b.reshape(1, C), dt_bias=vec(dt_bias), a_log=vec(a_log),
        dskip_x=jnp.repeat(d_skip, P).reshape(1, W), ssd_norm_g=ssd_norm_g.reshape(1, W),
        kv_norm_g=kv_norm_g, w_ukv=w_ukv.astype(BF16), q_norm_g=q_norm_g, k_norm_g=k_norm_g,
        w_branch=w_branch.astype(BF16), b_merge=b_merge, w_out=w_out.astype(BF16),
    )


def _layer(dims, p, h, layer, depth, k_stack, v_stack, sb_k_cache, sb_v_cache, ckv_cache, krope_cache, ssd_h0, conv_ctx):
    b, T, D = h.shape
    M = b * T
    H, P, G, N = dims["H"], dims["P"], dims["G"], dims["N"]
    HM, NOPE, R, V, KVL, HS, DS = dims["HM"], dims["NOPE"], dims["R"], dims["V"], dims["KVL"], dims["HS"], dims["DS"]
    hg = H // G
    past = 0 if sb_k_cache is None else sb_k_cache.shape[2]

    u = _rmsnorm(h.reshape(M, D), p["norm_g"])
    z = _matmul(u, p["w_z"], name="proj_z").reshape(b, T, -1)
    xbc = _matmul(u, p["w_xbc"], name="proj_xbc").reshape(b, T, -1)
    misc = _matmul(u, p["w_misc"], name="proj_misc").reshape(b, T, -1)
    pm = _matmul(u, p["w_mla"], name="proj_mla").reshape(b, T, -1)
    psb = _matmul(u, p["w_sb"], name="proj_sb").reshape(b, T, -1)
    k_stack = _matmul(u, p["w_k"], name="proj_sbk", stack=(k_stack, layer, depth))
    v_stack = _matmul(u, p["w_v"], name="proj_sbv", stack=(v_stack, layer, depth))
    gate_logit = _matmul(u, p["w_gate"], name="proj_gate")

    KW = p["conv_w"].shape[0]
    assert T >= KW - 1
    if ssd_h0 is None:
        h0_t = jnp.zeros((b, G, N, hg * P), F32)
        ctx = jnp.zeros((b, KW - 1, xbc.shape[-1]), F32)
    else:
        h0_t = jnp.swapaxes(ssd_h0.reshape(b, G, hg * P, N), 2, 3)
        ctx = conv_ctx
    y_a, st_t = _ssd_branch(z, xbc, misc, KVL // LANES, ctx, h0_t, p["conv_w"], p["conv_b"], p["dt_bias"],
                            p["a_log"], p["dskip_x"], p["ssd_norm_g"], H=H, P=P, G=G, N=N)
    ssd_new = jnp.swapaxes(st_t, 2, 3).reshape(b, H, P, N)
    conv_new = xbc[:, T - (KW - 1):, :]

    cos2, sin2 = _rope_tables(past, T, R)
    ckv, krope = _mla_new(misc, cos2, sin2, p["kv_norm_g"], KVL=KVL, R=R, kr_block=KVL // LANES + 1)
    q_full = _mla_q(pm, cos2, sin2, p["q_norm_g"], HM=HM, NOPE=NOPE, R=R)
    kv_args = dict(HM=HM, NOPE=NOPE, R=R, V=V)
    _, kb_new, kb_cache = _attn_blocks(T, past)
    k_new, vt_new = _mla_kv(ckv[None], krope[None], 0, kb_new, p["w_ukv"], p["k_norm_g"], **kv_args)
    k_old, vt_old = (_mla_kv(ckv_cache, krope_cache, layer, kb_cache, p["w_ukv"], p["k_norm_g"], **kv_args)
                     if past else (None, None))
    y_b = _mla_attn(q_full, k_new, vt_new, k_old, vt_old, pm, HM=HM, V=V, g_block0=(HM * (NOPE + R)) // V)

    y_c = _sb_attn(psb, k_stack.reshape(depth, b, T, HS * DS), v_stack.reshape(depth, b, T, HS * DS),
                   sb_k_cache, sb_v_cache, layer, HS=HS, DS=DS)

    mixed = _merge(y_a.reshape(M, -1), y_b.reshape(M, -1), y_c.reshape(M, -1), p["w_branch"], gate_logit, p["b_merge"])
    h_new = _matmul(mixed, p["w_out"], residual=h.reshape(M, D), name="out_proj").reshape(b, T, D)
    return h_new, k_stack, v_stack, (ckv, krope, ssd_new, conv_new)


def kernel(x_prompt, x_sample, cache_sb_k, cache_sb_v, cache_mla_ckv, cache_mla_krope, state_ssd, state_conv,
           norm_g, w_in, conv_w, conv_b, dt_bias, a_log, d_skip, ssd_norm_g, mla_kv_norm_g, w_ukv,
           mla_q_norm_g, mla_k_norm_g, w_branch, b_merge, w_out):
    depth = w_in.shape[0]
    H, P, N = state_ssd.shape[2:]
    C = conv_w.shape[-1]
    W = H * P
    HS, DS = cache_sb_k.shape[3:]
    KVL = cache_mla_ckv.shape[-1]
    R = cache_mla_krope.shape[-1]
    NOPE = mla_q_norm_g.shape[-1] - R
    HM = (w_ukv.shape[-1] - w_branch.shape[2]) // NOPE
    V = w_ukv.shape[-1] // HM - NOPE
    dims = dict(H=H, P=P, N=N, C=C, W=W, G=(C - W) // (2 * N), HS=HS, DS=DS, KVL=KVL, R=R, NOPE=NOPE, HM=HM, V=V)
    assert KVL % LANES == 0 and HM * V == HS * DS == W

    hp, hs = x_prompt, x_sample
    kp = vp = ks = vs = None
    outs_p, outs_s = [], []
    for l in range(depth):
        p = _prep_layer(dims, norm_g[l], w_in[l], conv_w[l], conv_b[l], dt_bias[l], a_log[l], d_skip[l], ssd_norm_g[l],
                        mla_kv_norm_g[l], w_ukv[l], mla_q_norm_g[l], mla_k_norm_g[l], w_branch[l], b_merge[l], w_out[l])
        hp, kp, vp, st_p = _layer(dims, p, hp, l, depth, kp, vp, None, None, None, None, None, None)
        outs_p.append(st_p)
        hs, ks, vs, st_s = _layer(dims, p, hs, l, depth, ks, vs, cache_sb_k, cache_sb_v, cache_mla_ckv,
                                  cache_mla_krope, state_ssd[l], state_conv[l])
        outs_s.append(st_s)
    kv_shape = lambda a, h: a.reshape(depth, h.shape[0], h.shape[1], HS, DS)
    stack_p = [jnp.stack(s) for s in zip(*outs_p)]
    stack_s = [jnp.stack(s) for s in zip(*outs_s)]
    return (hp, hs, kv_shape(kp, hp), kv_shape(vp, hp), *stack_p, kv_shape(ks, hs), kv_shape(vs, hs), *stack_s)
```

```python
import functools
import math

import jax
import jax.numpy as jnp
from jax import lax
from jax.experimental import pallas as pl
from jax.experimental.pallas import tpu as pltpu

CHUNK = 64
Q_BLOCK = 128
EPS = 1e-6
ROPE_THETA = 10000.0
LANES = 128
NEG_BIG = -1e30
LOG2E = 1.4426950408889634
UNDERFLOW_LOG2 = 160.0
VMEM_LIMIT_BYTES = 48 * 1024 * 1024

F32 = jnp.float32
BF16 = jnp.bfloat16


def _params(*sem, flags=None):
    return pltpu.CompilerParams(dimension_semantics=sem, vmem_limit_bytes=VMEM_LIMIT_BYTES, flags=flags)


def _pick(n, prefs):
    for p in prefs:
        if n % p == 0:
            return p
    return n


def _sigmoid(x):
    return 1.0 / (1.0 + jnp.exp(-x))


def _silu(x):
    return x * _sigmoid(x)


def _softplus(x):
    return jnp.maximum(x, 0.0) + jnp.log(1.0 + jnp.exp(-jnp.abs(x)))


def _rmsnorm_kernel(x_ref, g_ref, o_ref):
    x = x_ref[...]
    ms = jnp.mean(x * x, axis=-1, keepdims=True)
    o_ref[...] = (x * lax.rsqrt(ms + EPS) * g_ref[...]).astype(o_ref.dtype)


def _rmsnorm(x2d, g):
    m, d = x2d.shape
    tm = _pick(m, (512, 256, 128, 64, 32, 16))
    return pl.pallas_call(
        _rmsnorm_kernel,
        grid=(m // tm,),
        in_specs=[pl.BlockSpec((tm, d), lambda i: (i, 0)), pl.BlockSpec((1, d), lambda i: (0, 0))],
        out_specs=pl.BlockSpec((tm, d), lambda i: (i, 0)),
        out_shape=jax.ShapeDtypeStruct((m, d), BF16),
        compiler_params=_params("parallel"),
        name="rmsnorm",
    )(x2d, g.reshape(1, d))


def _mm_kernel(x_ref, w_ref, o_ref):
    o_ref[...] = jnp.dot(x_ref[...], w_ref[...], preferred_element_type=F32).astype(o_ref.dtype)


def _mm_res_kernel(x_ref, w_ref, r_ref, o_ref):
    o_ref[...] = r_ref[...] + jnp.dot(x_ref[...], w_ref[...], preferred_element_type=F32)


def _mm_into_kernel(x_ref, w_ref, prev_ref, o_ref):
    del prev_ref
    _mm_kernel(x_ref, w_ref, o_ref)


def _mm_first_kernel(x_ref, w_ref, o_ref, *, layer):
    slab = pl.program_id(0)

    @pl.when(slab == layer)
    def _():
        _mm_kernel(x_ref, w_ref, o_ref)

    @pl.when(slab != layer)
    def _():
        o_ref[...] = jnp.zeros_like(o_ref)


def _matmul(x, w, residual=None, name="matmul", stack=None):
    m, k = x.shape
    n = w.shape[1]
    tm = _pick(m, (1024, 512, 256, 128, 64, 32, 16))
    tn = _pick(n, (1024, 512, 256, 128))
    in_specs = [pl.BlockSpec((tm, k), lambda i, j: (i, 0)), pl.BlockSpec((k, tn), lambda i, j: (0, j))]
    args = [x, w]
    kern = _mm_kernel
    out_spec = pl.BlockSpec((tm, tn), lambda i, j: (i, j))
    out_shape = jax.ShapeDtypeStruct((m, n), F32)
    aliases = {}
    if residual is not None:
        in_specs.append(pl.BlockSpec((tm, tn), lambda i, j: (i, j)))
        args.append(residual)
        kern = _mm_res_kernel
    grid = (m // tm, n // tn)
    sem = ("parallel", "arbitrary")
    if stack is not None:
        assert residual is None
        prev, layer, depth = stack
        out_shape = jax.ShapeDtypeStruct((depth, m, n), F32)
        if prev is not None:
            out_spec = pl.BlockSpec((None, tm, tn), lambda i, j: (layer, i, j))
            in_specs.append(pl.BlockSpec(memory_space=pl.ANY))
            args.append(prev)
            kern = _mm_into_kernel
            aliases = {2: 0}
        else:
            here = lambda s, i: jnp.where(s == layer, i, 0)
            in_specs = [pl.BlockSpec((tm, k), lambda s, i, j: (here(s, i), 0)),
                        pl.BlockSpec((k, tn), lambda s, i, j: (0, here(s, j)))]
            out_spec = pl.BlockSpec((None, tm, tn), lambda s, i, j: (s, i, j))
            kern = functools.partial(_mm_first_kernel, layer=layer)
            grid = (depth,) + grid
            sem = ("parallel",) + sem
    return pl.pallas_call(
        kern,
        grid=grid,
        in_specs=in_specs,
        out_specs=out_spec,
        out_shape=out_shape,
        input_output_aliases=aliases,
        compiler_params=_params(*sem),
        name=name,
    )(*args)


def _ssd_kernel(z_ref, xbc_ref, dt_ref, ctx_ref, h0_ref, cw_ref, cb_ref, dtb_ref, alog_ref, dskip_ref, ng_ref,
                y_ref, st_ref, pad_scr, y_scr, *, L, H, P, G, N, KW):
    c = pl.program_id(1)
    W = H * P
    hg = H // G
    gw = hg * P
    top = 8 - (KW - 1)

    @pl.when(c == 0)
    def _():
        pad_scr[top:8, :] = ctx_ref[0]
        st_ref[0] = h0_ref[0]

    pad_scr[8:8 + L, :] = xbc_ref[0]
    conv = cb_ref[...] + cw_ref[0:1, :] * pad_scr[top:top + L, :]
    for j in range(1, KW):
        conv = conv + cw_ref[j:j + 1, :] * pad_scr[top + j:top + j + L, :]
    pad_scr[top:8, :] = pad_scr[8 + L - (KW - 1):8 + L, :]
    conv = _silu(conv)
    xs = conv[:, :W]
    bm = conv[:, W:W + G * N]
    cm = conv[:, W + G * N:]

    dt = _softplus(dt_ref[0] + dtb_ref[...])
    a = -jnp.exp(alog_ref[...])
    da = dt * a
    row = lax.broadcasted_iota(jnp.int32, (L, L), 0)
    col = lax.broadcasted_iota(jnp.int32, (L, L), 1)
    tril = col <= row
    cum = jnp.dot(tril.astype(F32), da, precision=lax.Precision.HIGHEST, preferred_element_type=F32)
    cum_t = cum.T
    dt_t = dt.T
    cum_last = cum[L - 1:L, :]
    wend = jnp.exp(cum_last - cum) * dt
    ecum = jnp.exp(cum)
    cdec = jnp.exp(cum_last)
    left = lax.broadcasted_iota(jnp.int32, (L, LANES), 1) < P
    left1 = lax.broadcasted_iota(jnp.int32, (1, LANES), 1) < P

    for g in range(G):
        bg = bm[:, g * N:(g + 1) * N].astype(BF16)
        cg = cm[:, g * N:(g + 1) * N].astype(BF16)
        cbm = lax.dot_general(cg, bg, (((1,), (1,)), ((), ())), preferred_element_type=F32)
        st = st_ref[0, g]
        yoff = jnp.dot(cg, st.astype(BF16), preferred_element_type=F32)
        xw_parts, cd_parts = [], []
        for j in range(hg // 2):
            h1 = g * hg + 2 * j
            h2 = h1 + 1
            xp = xs[:, h1 * P:h1 * P + LANES]
            yp = None
            for h, keep_left in ((h1, True), (h2, False)):
                diff = cum[:, h:h + 1] - cum_t[h:h + 1, :]
                w = cbm * jnp.exp(jnp.where(tril, diff, NEG_BIG)) * dt_t[h:h + 1, :]
                xh = jnp.where(left, xp, 0.0) if keep_left else jnp.where(left, 0.0, xp)
                t = jnp.dot(w.astype(BF16), xh.astype(BF16), preferred_element_type=F32)
                yp = t if yp is None else yp + t
            e = jnp.where(left, ecum[:, h1:h1 + 1], ecum[:, h2:h2 + 1])
            y_scr[:, h1 * P:h1 * P + LANES] = yp + e * yoff[:, j * LANES:(j + 1) * LANES]
            we = jnp.where(left, wend[:, h1:h1 + 1], wend[:, h2:h2 + 1])
            xw_parts.append(xp * we)
            cd_parts.append(jnp.where(left1, cdec[:, h1:h1 + 1], cdec[:, h2:h2 + 1]))
        xw = jnp.concatenate(xw_parts, axis=1).astype(BF16)
        cd = jnp.concatenate(cd_parts, axis=1)
        upd = lax.dot_general(bg, xw, (((0,), (0,)), ((), ())), preferred_element_type=F32)
        st_ref[0, g] = cd * st + upd

    y = (y_scr[...] + dskip_ref[...] * xs) * _silu(z_ref[0])
    for g in range(G):
        yg = y[:, g * gw:(g + 1) * gw]
        ms = jnp.mean(yg * yg, axis=-1, keepdims=True)
        y_ref[0, :, g * gw:(g + 1) * gw] = (
            yg * lax.rsqrt(ms + EPS) * ng_ref[:, g * gw:(g + 1) * gw]).astype(y_ref.dtype)


def _ssd_branch(z, xbc, misc, dt_block, ctx, h0_t, cw, cb, dtb, alog, dskip_x, ng, *, H, P, G, N):
    b, T, W = z.shape
    C = xbc.shape[-1]
    KW = cw.shape[0]
    L = min(CHUNK, T)
    hg = H // G
    gw = hg * P
    assert T % L == 0 and L >= KW - 1 and hg % 2 == 0 and 2 * P == LANES and H <= LANES
    kern = functools.partial(_ssd_kernel, L=L, H=H, P=P, G=G, N=N, KW=KW)
    const2 = lambda bi, c: (0, 0)
    return pl.pallas_call(
        kern,
        grid=(b, T // L),
        in_specs=[
            pl.BlockSpec((1, L, W), lambda bi, c: (bi, c, 0)),
            pl.BlockSpec((1, L, C), lambda bi, c: (bi, c, 0)),
            pl.BlockSpec((1, L, LANES), lambda bi, c: (bi, c, dt_block)),
            pl.BlockSpec((1, KW - 1, C), lambda bi, c: (bi, 0, 0)),
            pl.BlockSpec((1, G, N, gw), lambda bi, c: (bi, 0, 0, 0)),
            pl.BlockSpec((KW, C), const2),
            pl.BlockSpec((1, C), const2),
            pl.BlockSpec((1, LANES), const2),
            pl.BlockSpec((1, LANES), const2),
            pl.BlockSpec((1, W), const2),
            pl.BlockSpec((1, W), const2),
        ],
        out_specs=[
            pl.BlockSpec((1, L, W), lambda bi, c: (bi, c, 0)),
            pl.BlockSpec((1, G, N, gw), lambda bi, c: (bi, 0, 0, 0)),
        ],
        out_shape=[jax.ShapeDtypeStruct((b, T, W), BF16), jax.ShapeDtypeStruct((b, G, N, gw), F32)],
        scratch_shapes=[pltpu.VMEM((8 + L, C), F32), pltpu.VMEM((L, W), F32)],
        compiler_params=_params("parallel", "arbitrary"),
        name="ssd_scan",
    )(z, xbc, misc, ctx, h0_t, cw, cb, dtb, alog, dskip_x, ng)


def _swap_halves(x, half):
    lane = lax.broadcasted_iota(jnp.int32, x.shape, 1)
    first = (lane % (2 * half)) < half
    return jnp.where(first, pltpu.roll(x, LANES - half, 1), pltpu.roll(x, half, 1))


def _mla_new_kernel(ckv_ref, kr_ref, cos_ref, sin_ref, g_ref, ckv_out, kr_out, *, R):
    x = ckv_ref[0]
    ms = jnp.mean(x * x, axis=-1, keepdims=True)
    ckv_out[0] = x * lax.rsqrt(ms + EPS) * g_ref[...]
    kr = kr_ref[0]
    rot = kr * cos_ref[...] + _swap_halves(kr, R // 2) * sin_ref[...]
    kr_out[0] = rot[:, :R]


def _mla_new(misc, cos2, sin2, kv_norm_g, *, KVL, R, kr_block):
    b, T, _ = misc.shape
    tb = _pick(T, (256, 128, 64))
    return pl.pallas_call(
        functools.partial(_mla_new_kernel, R=R),
        grid=(b, T // tb),
        in_specs=[
            pl.BlockSpec((1, tb, KVL), lambda bi, t: (bi, t, 0)),
            pl.BlockSpec((1, tb, LANES), lambda bi, t: (bi, t, kr_block)),
            pl.BlockSpec((tb, LANES), lambda bi, t: (t, 0)),
            pl.BlockSpec((tb, LANES), lambda bi, t: (t, 0)),
            pl.BlockSpec((1, KVL), lambda bi, t: (0, 0)),
        ],
        out_specs=[
            pl.BlockSpec((1, tb, KVL), lambda bi, t: (bi, t, 0)),
            pl.BlockSpec((1, tb, R), lambda bi, t: (bi, t, 0)),
        ],
        out_shape=[jax.ShapeDtypeStruct((b, T, KVL), F32), jax.ShapeDtypeStruct((b, T, R), F32)],
        compiler_params=_params("parallel", "parallel"),
        name="mla_new_kv",
    )(misc, misc, cos2, sin2, kv_norm_g.reshape(1, KVL))


def _mla_q_kernel(qn_ref, qr_ref, cos_ref, sin_ref, gn_ref, gr_ref, o_ref, *, HM, NOPE, R, scale):
    lane = lax.broadcasted_iota(jnp.int32, (qn_ref.shape[1], LANES), 1)
    left = lane < R
    for j in range(HM // 2):
        qr = qr_ref[0, :, j * LANES:(j + 1) * LANES]
        rot = qr * cos_ref[...] + _swap_halves(qr, R // 2) * sin_ref[...]
        sq = rot * rot
        for h, keep_left in ((2 * j, True), (2 * j + 1, False)):
            qn = qn_ref[0, :, h * NOPE:(h + 1) * NOPE]
            sel = left if keep_left else jnp.logical_not(left)
            ss = jnp.sum(qn * qn, axis=-1, keepdims=True) + jnp.sum(jnp.where(sel, sq, 0.0), axis=-1, keepdims=True)
            inv = lax.rsqrt(ss * (1.0 / (NOPE + R)) + EPS) * scale
            o_ref[0, :, h * 2 * LANES:h * 2 * LANES + NOPE] = (qn * inv * gn_ref[...]).astype(o_ref.dtype)
            o_ref[0, :, h * 2 * LANES + NOPE:(h + 1) * 2 * LANES] = (
                jnp.where(sel, rot * inv * gr_ref[...], 0.0)).astype(o_ref.dtype)


def _mla_q(pm, cos2, sin2, q_norm_g, *, HM, NOPE, R):
    b, T, _ = pm.shape
    assert NOPE == LANES and 2 * R == LANES and HM % 2 == 0
    tb = _pick(T, (256, 128, 64))
    scale = LOG2E / math.sqrt(NOPE + R)
    gn = q_norm_g[:NOPE].reshape(1, NOPE)
    gr = jnp.tile(q_norm_g[NOPE:], 2).reshape(1, LANES)
    nope_w, rope_w = HM * NOPE, HM * R
    assert nope_w % rope_w == 0
    return pl.pallas_call(
        functools.partial(_mla_q_kernel, HM=HM, NOPE=NOPE, R=R, scale=scale),
        grid=(b, T // tb),
        in_specs=[
            pl.BlockSpec((1, tb, nope_w), lambda bi, t: (bi, t, 0)),
            pl.BlockSpec((1, tb, rope_w), lambda bi, t: (bi, t, nope_w // rope_w)),
            pl.BlockSpec((tb, LANES), lambda bi, t: (t, 0)),
            pl.BlockSpec((tb, LANES), lambda bi, t: (t, 0)),
            pl.BlockSpec((1, NOPE), lambda bi, t: (0, 0)),
            pl.BlockSpec((1, LANES), lambda bi, t: (0, 0)),
        ],
        out_specs=pl.BlockSpec((1, tb, HM * 2 * LANES), lambda bi, t: (bi, t, 0)),
        out_shape=jax.ShapeDtypeStruct((b, T, HM * 2 * LANES), BF16),
        compiler_params=_params("parallel", "parallel"),
        name="mla_q_prep",
    )(pm, pm, cos2, sin2, gn, gr)


def _mla_kv_kernel(ckv_ref, kr_ref, w_ref, gn_ref, gr_ref, k_ref, vt_ref, *, HM, NOPE, R, V):
    ckv = ckv_ref[0, 0].astype(BF16)
    kv = jnp.dot(ckv, w_ref[...], preferred_element_type=F32)
    kr = kr_ref[0, 0]
    ss_r = jnp.sum(kr * kr, axis=-1, keepdims=True)
    krg = kr * gr_ref[...]
    zero = jnp.zeros_like(krg)
    kr_even = jnp.concatenate([krg, zero], axis=1)
    kr_odd = jnp.concatenate([zero, krg], axis=1)
    for h in range(HM):
        base = h * (NOPE + V)
        kn = kv[:, base:base + NOPE]
        ss = jnp.sum(kn * kn, axis=-1, keepdims=True) + ss_r
        inv = lax.rsqrt(ss * (1.0 / (NOPE + R)) + EPS)
        k_ref[0, :, h * 2 * LANES:h * 2 * LANES + NOPE] = (kn * inv * gn_ref[...]).astype(k_ref.dtype)
        k_ref[0, :, h * 2 * LANES + NOPE:(h + 1) * 2 * LANES] = (
            (kr_even if h % 2 == 0 else kr_odd) * inv).astype(k_ref.dtype)
        vt_ref[0, 0, h * V:(h + 1) * V, :] = kv[:, base + NOPE:base + NOPE + V].T.astype(vt_ref.dtype)


def _mla_kv(ckv, krope, layer, ts, w_ukv, k_norm_g, *, HM, NOPE, R, V):
    _, b, S, KVL = ckv.shape
    assert S % ts == 0
    gn = k_norm_g[:NOPE].reshape(1, NOPE)
    gr = k_norm_g[NOPE:].reshape(1, R)
    return pl.pallas_call(
        functools.partial(_mla_kv_kernel, HM=HM, NOPE=NOPE, R=R, V=V),
        grid=(b, S // ts),
        in_specs=[
            pl.BlockSpec((1, 1, ts, KVL), lambda bi, s: (layer, bi, s, 0)),
            pl.BlockSpec((1, 1, ts, R), lambda bi, s: (layer, bi, s, 0)),
            pl.BlockSpec((KVL, HM * (NOPE + V)), lambda bi, s: (0, 0)),
            pl.BlockSpec((1, NOPE), lambda bi, s: (0, 0)),
            pl.BlockSpec((1, R), lambda bi, s: (0, 0)),
        ],
        out_specs=[
            pl.BlockSpec((1, ts, HM * 2 * LANES), lambda bi, s: (bi, s, 0)),
            pl.BlockSpec((1, 1, HM * V, ts), lambda bi, s: (bi, s, 0, 0)),
        ],
        out_shape=[jax.ShapeDtypeStruct((b, S, HM * 2 * LANES), BF16),
                   jax.ShapeDtypeStruct((b, S // ts, HM * V, ts), BF16)],
        compiler_params=_params("parallel", "parallel"),
        name="mla_kv_prep",
    )(ckv, krope, w_ukv, gn, gr)


def _attn_blocks(T, past):
    qb = min(T, 4 * Q_BLOCK)
    kb = min(qb, 2 * Q_BLOCK)
    kbc = _pick(past, (2 * Q_BLOCK, Q_BLOCK, CHUNK)) if past else kb
    assert T % qb == 0 and qb % kb == 0 and kb % CHUNK == 0 and past % CHUNK == 0 and past % kbc == 0
    assert T % Q_BLOCK == 0 or T <= Q_BLOCK
    return qb, kb, kbc


def _mla_scores(q, k):
    return lax.dot_general(k, q, (((1,), (1,)), ((), ())), preferred_element_type=F32)


def _mla_accumulate(ss, vts, carries):
    m_news = [jnp.maximum(m, jnp.max(s, axis=0, keepdims=True)) for s, (m, _, _) in zip(ss, carries)]
    ps = [jnp.exp2(s - mn) for s, mn in zip(ss, m_news)]
    alphas = [jnp.exp2(m - mn) for (m, _, _), mn in zip(carries, m_news)]
    pvs = [jnp.dot(vt, p.astype(BF16), preferred_element_type=F32) for vt, p in zip(vts, ps)]
    return tuple((mn, a * l + jnp.sum(p, axis=0, keepdims=True), a * acc + pv)
                 for mn, a, p, pv, (_, l, acc) in zip(m_news, alphas, ps, pvs, carries))


def _mla_attn_kernel(*refs, QB, KB, KBC, DQ, n_cache, single, shift):
    if n_cache:
        q_ref, kn_ref, vtn_ref, kc_ref, vtc_ref, g_ref, o_ref = refs
    else:
        q_ref, kn_ref, vtn_ref, g_ref, o_ref = refs
    qi = pl.program_id(2)
    nh = q_ref.shape[-1] // DQ
    dv = vtn_ref.shape[2] // nh
    qcols = [slice(hh * DQ, (hh + 1) * DQ) for hh in range(nh)]
    vrows = [slice(hh * dv, (hh + 1) * dv) for hh in range(nh)]
    qs = [q_ref[0, :, c] for c in qcols]
    carries = tuple((jnp.full((1, QB), NEG_BIG, F32), jnp.zeros((1, QB), F32), jnp.zeros((dv, QB), F32))
                    for _ in range(nh))
    nd = QB // KB
    for kk in range(nd):
        r0 = kk * KB
        rows = QB - r0
        start = r0 if single else pl.multiple_of(qi * QB + r0, KB)
        blk = kk if single else qi * nd + kk
        kchunk = lax.shift_right_logical(lax.broadcasted_iota(jnp.int32, (KB, rows), 0), shift)
        rchunk = lax.shift_right_logical(lax.broadcasted_iota(jnp.int32, (KB, rows), 1), shift)
        mask = kchunk <= rchunk
        ss = [jnp.where(mask, _mla_scores(q[r0:], kn_ref[0, pl.ds(start, KB), c]), NEG_BIG) for q, c in zip(qs, qcols)]
        new = _mla_accumulate(ss, [vtn_ref[0, blk, r, :] for r in vrows],
                              [tuple(a[:, r0:] for a in carry) for carry in carries])
        if r0:
            new = tuple(tuple(jnp.concatenate([a[:, :r0], b], axis=1) for a, b in zip(carry, nw))
                        for carry, nw in zip(carries, new))
        carries = new

    def sweep(carries, k_ref, vt_ref, n, per_tile, kb):
        width = per_tile * kb

        def scores(j):
            start = pl.multiple_of(j * width, width)
            return [_mla_scores(q, k_ref[0, pl.ds(start, width), c]) for q, c in zip(qs, qcols)]

        def values(j):
            return [jnp.concatenate([vt_ref[0, j * per_tile + u, r, :] for u in range(per_tile)], axis=1) for r in vrows]

        if not isinstance(n, int):
            return lax.fori_loop(0, n, lambda j, c: _mla_accumulate(scores(j), values(j), c), carries)

        def body(j, state):
            s_next = scores(j + 1)
            return s_next, _mla_accumulate(state[0], values(j), state[1])

        state = lax.fori_loop(0, n - 1, body, (scores(0), carries))
        return _mla_accumulate(state[0], values(n - 1), state[1])

    if not single:
        per_new = 2 if nd % 2 == 0 else 1
        carries = sweep(carries, kn_ref, vtn_ref, (qi * nd) // per_new, per_new, KB)
    if n_cache:
        per_cache = 2 if n_cache % 2 == 0 else 1
        carries = sweep(carries, kc_ref, vtc_ref, n_cache // per_cache, per_cache, KBC)
    for hh, (_, l, acc) in enumerate(carries):
        cols = slice(hh * dv, (hh + 1) * dv)
        o_ref[0, :, cols] = ((acc * (1.0 / l)).T * _silu(g_ref[0, :, cols])).astype(o_ref.dtype)


def _mla_attn(q_full, k_new, vt_new, k_cache, vt_cache, pm, *, HM, V, g_block0):
    b, T, _ = q_full.shape
    assert CHUNK & (CHUNK - 1) == 0 and V == LANES
    past = 0 if k_cache is None else k_cache.shape[1]
    QB, KB, KBC = _attn_blocks(T, past)
    assert vt_new.shape[-1] == KB and (not past or vt_cache.shape[-1] == KBC)
    hp = next(n for n in (4, 2, 1) if HM % n == 0 and g_block0 % n == 0)
    dq = 2 * LANES
    kern = functools.partial(_mla_attn_kernel, QB=QB, KB=KB, KBC=KBC, DQ=dq, n_cache=past // KBC, single=T == QB,
                             shift=CHUNK.bit_length() - 1)
    in_specs = [
        pl.BlockSpec((1, QB, hp * dq), lambda bi, h, t: (bi, t, h)),
        pl.BlockSpec((1, T, hp * dq), lambda bi, h, t: (bi, 0, h)),
        pl.BlockSpec((1, T // KB, hp * V, KB), lambda bi, h, t: (bi, 0, h, 0)),
    ]
    args = [q_full, k_new, vt_new]
    if past:
        in_specs += [pl.BlockSpec((1, past, hp * dq), lambda bi, h, t: (bi, 0, h)),
                     pl.BlockSpec((1, past // KBC, hp * V, KBC), lambda bi, h, t: (bi, 0, h, 0))]
        args += [k_cache, vt_cache]
    in_specs.append(pl.BlockSpec((1, QB, hp * V), lambda bi, h, t: (bi, t, g_block0 // hp + h)))
    args.append(pm)
    return pl.pallas_call(
        kern,
        grid=(b, HM // hp, T // QB),
        in_specs=in_specs,
        out_specs=pl.BlockSpec((1, QB, hp * V), lambda bi, h, t: (bi, t, h)),
        out_shape=jax.ShapeDtypeStruct((b, T, HM * V), BF16),
        compiler_params=_params("parallel", "parallel", "arbitrary"),
        name="mla_attention",
    )(*args)


def _strict_lower(n):
    row = lax.broadcasted_iota(jnp.int32, (2 * n, n), 0)
    col = lax.broadcasted_iota(jnp.int32, (2 * n, n), 1)
    return (jnp.where(row >= n, row - n, row) > col).astype(BF16)


def _sb_tiles(qs, ks, vs, carries, tri2, causal):
    zs = [lax.dot_general(q, k, (((1,), (1,)), ((), ())), preferred_element_type=F32) for q, k in zip(qs, ks)]
    mask = None
    if causal:
        mask = lax.broadcasted_iota(jnp.int32, zs[0].shape, 1) < lax.broadcasted_iota(jnp.int32, zs[0].shape, 0)
    sps, logsigs, stacked = [], [], []
    for z in zs:
        neg_abs = lax.bitcast_convert_type(lax.bitcast_convert_type(z, jnp.uint32) | jnp.uint32(0x80000000), F32)
        sp = jnp.log2(1.0 + jnp.exp2(neg_abs)) + jnp.maximum(z, 0.0)
        logsigs.append(z - sp)
        if causal:
            sp = jnp.where(mask, sp, 0.0)
        hi = sp.astype(BF16)
        lo = (sp - hi.astype(F32)).astype(BF16)
        sps.append(sp)
        stacked.append(jnp.concatenate([hi, lo], axis=1))
    laters = [jnp.dot(x, tri2, preferred_element_type=F32) for x in stacked]
    atts = []
    for logsig, later, (run, _) in zip(logsigs, laters, carries):
        att = jnp.exp2(logsig - later - run)
        if causal:
            att = jnp.where(mask, att, 0.0)
        atts.append(att.astype(BF16))
    return tuple((run + jnp.sum(sp, axis=-1, keepdims=True), acc + jnp.dot(att, v, preferred_element_type=F32))
                 for sp, att, v, (run, acc) in zip(sps, atts, vs, carries))


def _sb_tile(q, k, v, run, acc, tri2, causal):
    return _sb_tiles([q], [k], [v], [(run, acc)], tri2, causal)[0]


def _sb_attn_kernel(*refs, QB, KB, KBC, D, n_cache, layer, single, scale):
    if n_cache:
        q_ref, kn_ref, vn_ref, kc_hbm, vc_hbm, g_ref, o_ref, kbuf, vbuf, sem = refs
        bi, head = pl.program_id(0), pl.program_id(1)

        def cache_copies(i):
            start = pl.multiple_of((n_cache - 1 - i) * KBC, KBC)
            slot = i % 2
            return (pltpu.make_async_copy(kc_hbm.at[layer, bi, pl.ds(start, KBC), head, :], kbuf.at[slot], sem.at[0, slot]),
                    pltpu.make_async_copy(vc_hbm.at[layer, bi, pl.ds(start, KBC), head, :], vbuf.at[slot], sem.at[1, slot]))

        for c in cache_copies(0):
            c.start()
    else:
        q_ref, kn_ref, vn_ref, g_ref, o_ref = refs
    qi = pl.program_id(2)
    heads = [slice(hh * D, (hh + 1) * D) for hh in range(q_ref.shape[-1] // D)]
    qs = [(q_ref[0, :, hs] * scale).astype(BF16) for hs in heads]
    tri = _strict_lower(KB)
    carries = tuple((jnp.zeros((QB, 1), F32), jnp.zeros((QB, D), F32)) for _ in heads)
    nd = QB // KB
    for kk in reversed(range(nd)):
        r0 = kk * KB
        start = r0 if single else pl.multiple_of(qi * QB + r0, KB)
        ks = [kn_ref[0, 0, pl.ds(start, KB), hs].astype(BF16) for hs in heads]
        vs = [vn_ref[0, 0, pl.ds(start, KB), hs].astype(BF16) for hs in heads]
        new = _sb_tiles([q[r0:] for q in qs], ks, vs, [(run[r0:], acc[r0:]) for run, acc in carries], tri, True)
        if r0:
            new = tuple((jnp.concatenate([run[:r0], rn], axis=0), jnp.concatenate([acc[:r0], an], axis=0))
                        for (run, acc), (rn, an) in zip(carries, new))
        carries = new

    def sweep(carries, n, load, tri_kb):
        def cond(state):
            i, carries = state
            low = functools.reduce(jnp.minimum, [jnp.min(run) for run, _ in carries])
            return jnp.logical_and(i < n, low < UNDERFLOW_LOG2)

        def body(state):
            i, carries = state
            ks, vs = load(i)
            return i + 1, _sb_tiles(qs, [k.astype(BF16) for k in ks], [v.astype(BF16) for v in vs], carries, tri_kb, False)

        return lax.while_loop(cond, body, (jnp.int32(0), carries))

    n_new = qi * nd

    def load_new(i):
        start = pl.multiple_of((n_new - 1 - i) * KB, KB)
        return ([kn_ref[0, 0, pl.ds(start, KB), hs] for hs in heads], [vn_ref[0, 0, pl.ds(start, KB), hs] for hs in heads])

    _, carries = sweep(carries, n_new, load_new, tri)
    if n_cache:
        def load_cache(i):
            for c in cache_copies(i):
                c.wait()

            @pl.when(i + 1 < n_cache)
            def _():
                for c in cache_copies(i + 1):
                    c.start()

            return [kbuf[i % 2]], [vbuf[i % 2]]

        used, carries = sweep(carries, n_cache, load_cache, tri if KBC == KB else _strict_lower(KBC))

        @pl.when(used < n_cache)
        def _():
            for c in cache_copies(used):
                c.wait()

    for hs, (_, acc) in zip(heads, carries):
        o_ref[0, :, hs] = (acc * _silu(g_ref[0, :, hs])).astype(o_ref.dtype)


def _sb_attn(psb, k_new, v_new, k_cache, v_cache, layer, *, HS, DS):
    b, T, _ = psb.shape
    assert DS == LANES
    past = 0 if k_cache is None else k_cache.shape[2]
    QB, KB, KBC = _attn_blocks(T, past)
    hp = 1 if past else next(n for n in (4, 2, 1) if HS % n == 0)
    w = hp * DS
    kern = functools.partial(_sb_attn_kernel, QB=QB, KB=KB, KBC=KBC, D=DS, n_cache=past // KBC, layer=layer,
                             single=T == QB, scale=LOG2E / math.sqrt(DS))
    in_specs = [
        pl.BlockSpec((1, QB, w), lambda bi, h, t: (bi, t, h)),
        pl.BlockSpec((1, 1, T, w), lambda bi, h, t: (layer, bi, 0, h)),
        pl.BlockSpec((1, 1, T, w), lambda bi, h, t: (layer, bi, 0, h)),
    ]
    args = [psb, k_new, v_new]
    scratch = []
    if past:
        in_specs += [pl.BlockSpec(memory_space=pl.ANY)] * 2
        args += [k_cache, v_cache]
        scratch = [pltpu.VMEM((2, KBC, DS), k_cache.dtype), pltpu.VMEM((2, KBC, DS), v_cache.dtype),
                   pltpu.SemaphoreType.DMA((2, 2))]
    in_specs.append(pl.BlockSpec((1, QB, w), lambda bi, h, t: (bi, t, HS // hp + h)))
    args.append(psb)
    return pl.pallas_call(
        kern,
        grid=(b, HS // hp, T // QB),
        in_specs=in_specs,
        out_specs=pl.BlockSpec((1, QB, w), lambda bi, h, t: (bi, t, h)),
        out_shape=jax.ShapeDtypeStruct((b, T, HS * DS), BF16),
        scratch_shapes=scratch,
        compiler_params=_params("parallel", "parallel", "arbitrary"),
        name="sb_attention",
    )(*args)


def _merge_kernel(ya_ref, yb_ref, yc_ref, w_ref, g0_ref, g1_ref, g2_ref, bm_ref, o_ref):
    acc = None
    for i, (y_ref, g_ref) in enumerate(((ya_ref, g0_ref), (yb_ref, g1_ref), (yc_ref, g2_ref))):
        gate = _sigmoid(g_ref[...] + bm_ref[i])
        t = gate * jnp.dot(y_ref[...], w_ref[i], preferred_element_type=F32)
        acc = t if acc is None else acc + t
    o_ref[...] = acc.astype(o_ref.dtype)


def _merge(ya, yb, yc, w_branch, gate_logit, b_merge):
    m, k = ya.shape
    nb, _, d = w_branch.shape
    tm = _pick(m, (512, 256, 128, 64, 32, 16))
    tn = _pick(d, (512, 256, 128))
    nt = d // tn
    yspec = pl.BlockSpec((tm, k), lambda i, j: (i, 0))
    gspecs = [pl.BlockSpec((tm, tn), functools.partial(lambda i, j, br: (i, br * nt + j), br=br)) for br in range(nb)]
    return pl.pallas_call(
        _merge_kernel,
        grid=(m // tm, nt),
        in_specs=[yspec, yspec, yspec, pl.BlockSpec((nb, k, tn), lambda i, j: (0, 0, j))] + gspecs
        + [pl.BlockSpec((nb, 1, tn), lambda i, j: (0, 0, j))],
        out_specs=pl.BlockSpec((tm, tn), lambda i, j: (i, j)),
        out_shape=jax.ShapeDtypeStruct((m, d), BF16),
        compiler_params=_params("parallel", "arbitrary"),
        name="branch_merge",
    )(ya, yb, yc, w_branch, gate_logit, gate_logit, gate_logit, b_merge.reshape(nb, 1, d))


def _rope_tables(past, T, R):
    half = R // 2
    freq = ROPE_THETA ** (-jnp.arange(half, dtype=F32) / half)
    ang = (past + jnp.arange(T)).astype(F32)[:, None] * freq[None, :]
    cos, sin = jnp.cos(ang), jnp.sin(ang)
    reps = LANES // R
    return jnp.tile(jnp.concatenate([cos, cos], axis=1), (1, reps)), jnp.tile(jnp.concatenate([-sin, sin], axis=1), (1, reps))


def _prep_layer(dims, norm_g, w_in, conv_w, conv_b, dt_bias, a_log, d_skip, ssd_norm_g, kv_norm_g, w_ukv,
                q_norm_g, k_norm_g, w_branch, b_merge, w_out):
    D = w_in.shape[0]
    W, C, H, P = dims["W"], dims["C"], dims["H"], dims["P"]
    HM, NOPE, R, KVL, HS, DS = dims["HM"], dims["NOPE"], dims["R"], dims["KVL"], dims["HS"], dims["DS"]
    sizes = (W, C, H, HM * (NOPE + R), KVL, R, HM * dims["V"], HS * DS, HS * DS, HS * DS, HS * DS, w_branch.shape[0] * D)
    offs = [0]
    for s in sizes:
        offs.append(offs[-1] + s)
    col = lambda i: w_in[:, offs[i]:offs[i + 1]]
    pad_to = lambda w, n: jnp.pad(w, ((0, 0), (0, n - w.shape[1])))
    wq = col(3).reshape(D, HM, NOPE + R)
    w_mla = jnp.concatenate([wq[:, :, :NOPE].reshape(D, HM * NOPE), wq[:, :, NOPE:].reshape(D, HM * R), col(6)], axis=1)
    w_misc = jnp.concatenate([col(4), pad_to(col(2), LANES), pad_to(col(5), LANES)], axis=1)
    vec = lambda v: jnp.pad(v, (0, LANES - v.shape[0])).reshape(1, LANES)
    return dict(
        norm_g=norm_g,
        w_z=col(0).astype(BF16), w_xbc=col(1).astype(BF16), w_misc=w_misc.astype(BF16), w_mla=w_mla.astype(BF16),
        w_sb=jnp.concatenate([col(7), col(10)], axis=1).astype(BF16),
        w_k=col(8).astype(BF16), w_v=col(9).astype(BF16), w_gate=col(11).astype(BF16),
        conv_w=conv_w, conv_b=conv_b.reshape(1, C), dt_bias=vec(dt_bias), a_log=vec(a_log),
        dskip_x=jnp.repeat(d_skip, P).reshape(1, W), ssd_norm_g=ssd_norm_g.reshape(1, W),
        kv_norm_g=kv_norm_g, w_ukv=w_ukv.astype(BF16), q_norm_g=q_norm_g, k_norm_g=k_norm_g,
        w_branch=w_branch.astype(BF16), b_merge=b_merge, w_out=w_out.astype(BF16),
    )


def _layer(dims, p, h, layer, depth, k_stack, v_stack, sb_k_cache, sb_v_cache, ckv_cache, krope_cache, ssd_h0, conv_ctx):
    b, T, D = h.shape
    M = b * T
    H, P, G, N = dims["H"], dims["P"], dims["G"], dims["N"]
    HM, NOPE, R, V, KVL, HS, DS = dims["HM"], dims["NOPE"], dims["R"], dims["V"], dims["KVL"], dims["HS"], dims["DS"]
    hg = H // G
    past = 0 if sb_k_cache is None else sb_k_cache.shape[2]

    u = _rmsnorm(h.reshape(M, D), p["norm_g"])
    z = _matmul(u, p["w_z"], name="proj_z").reshape(b, T, -1)
    xbc = _matmul(u, p["w_xbc"], name="proj_xbc").reshape(b, T, -1)
    misc = _matmul(u, p["w_misc"], name="proj_misc").reshape(b, T, -1)
    pm = _matmul(u, p["w_mla"], name="proj_mla").reshape(b, T, -1)
    psb = _matmul(u, p["w_sb"], name="proj_sb").reshape(b, T, -1)
    k_stack = _matmul(u, p["w_k"], name="proj_sbk", stack=(k_stack, layer, depth))
    v_stack = _matmul(u, p["w_v"], name="proj_sbv", stack=(v_stack, layer, depth))
    gate_logit = _matmul(u, p["w_gate"], name="proj_gate")

    KW = p["conv_w"].shape[0]
    assert T >= KW - 1
    if ssd_h0 is None:
        h0_t = jnp.zeros((b, G, N, hg * P), F32)
        ctx = jnp.zeros((b, KW - 1, xbc.shape[-1]), F32)
    else:
        h0_t = jnp.swapaxes(ssd_h0.reshape(b, G, hg * P, N), 2, 3)
        ctx = conv_ctx
    y_a, st_t = _ssd_branch(z, xbc, misc, KVL // LANES, ctx, h0_t, p["conv_w"], p["conv_b"], p["dt_bias"],
                            p["a_log"], p["dskip_x"], p["ssd_norm_g"], H=H, P=P, G=G, N=N)
    ssd_new = jnp.swapaxes(st_t, 2, 3).reshape(b, H, P, N)
    conv_new = xbc[:, T - (KW - 1):, :]

    cos2, sin2 = _rope_tables(past, T, R)
    ckv, krope = _mla_new(misc, cos2, sin2, p["kv_norm_g"], KVL=KVL, R=R, kr_block=KVL // LANES + 1)
    q_full = _mla_q(pm, cos2, sin2, p["q_norm_g"], HM=HM, NOPE=NOPE, R=R)
    kv_args = dict(HM=HM, NOPE=NOPE, R=R, V=V)
    _, kb_new, kb_cache = _attn_blocks(T, past)
    k_new, vt_new = _mla_kv(ckv[None], krope[None], 0, kb_new, p["w_ukv"], p["k_norm_g"], **kv_args)
    k_old, vt_old = (_mla_kv(ckv_cache, krope_cache, layer, kb_cache, p["w_ukv"], p["k_norm_g"], **kv_args)
                     if past else (None, None))
    y_b = _mla_attn(q_full, k_new, vt_new, k_old, vt_old, pm, HM=HM, V=V, g_block0=(HM * (NOPE + R)) // V)

    y_c = _sb_attn(psb, k_stack.reshape(depth, b, T, HS * DS), v_stack.reshape(depth, b, T, HS * DS),
                   sb_k_cache, sb_v_cache, layer, HS=HS, DS=DS)

    mixed = _merge(y_a.reshape(M, -1), y_b.reshape(M, -1), y_c.reshape(M, -1), p["w_branch"], gate_logit, p["b_merge"])
    h_new = _matmul(mixed, p["w_out"], residual=h.reshape(M, D), name="out_proj").reshape(b, T, D)
    return h_new, k_stack, v_stack, (ckv, krope, ssd_new, conv_new)


def kernel(x_prompt, x_sample, cache_sb_k, cache_sb_v, cache_mla_ckv, cache_mla_krope, state_ssd, state_conv,
           norm_g, w_in, conv_w, conv_b, dt_bias, a_log, d_skip, ssd_norm_g, mla_kv_norm_g, w_ukv,
           mla_q_norm_g, mla_k_norm_g, w_branch, b_merge, w_out):
    depth = w_in.shape[0]
    H, P, N = state_ssd.shape[2:]
    C = conv_w.shape[-1]
    W = H * P
    HS, DS = cache_sb_k.shape[3:]
    KVL = cache_mla_ckv.shape[-1]
    R = cache_mla_krope.shape[-1]
    NOPE = mla_q_norm_g.shape[-1] - R
    HM = (w_ukv.shape[-1] - w_branch.shape[2]) // NOPE
    V = w_ukv.shape[-1] // HM - NOPE
    dims = dict(H=H, P=P, N=N, C=C, W=W, G=(C - W) // (2 * N), HS=HS, DS=DS, KVL=KVL, R=R, NOPE=NOPE, HM=HM, V=V)
    assert KVL % LANES == 0 and HM * V == HS * DS == W

    hp, hs = x_prompt, x_sample
    kp = vp = ks = vs = None
    outs_p, outs_s = [], []
    for l in range(depth):
        p = _prep_layer(dims, norm_g[l], w_in[l], conv_w[l], conv_b[l], dt_bias[l], a_log[l], d_skip[l], ssd_norm_g[l],
                        mla_kv_norm_g[l], w_ukv[l], mla_q_norm_g[l], mla_k_norm_g[l], w_branch[l], b_merge[l], w_out[l])
        hp, kp, vp, st_p = _layer(dims, p, hp, l, depth, kp, vp, None, None, None, None, None, None)
        outs_p.append(st_p)
        hs, ks, vs, st_s = _layer(dims, p, hs, l, depth, ks, vs, cache_sb_k, cache_sb_v, cache_mla_ckv,
                                  cache_mla_krope, state_ssd[l], state_conv[l])
        outs_s.append(st_s)
    kv_shape = lambda a, h: a.reshape(depth, h.shape[0], h.shape[1], HS, DS)
    stack_p = [jnp.stack(s) for s in zip(*outs_p)]
    stack_s = [jnp.stack(s) for s in zip(*outs_s)]
    return (hp, hs, kv_shape(kp, hp), kv_shape(vp, hp), *stack_p, kv_shape(ks, hs), kv_shape(vs, hs), *stack_s)
```
